```python
import jax, jax.numpy as jnp
from jax import lax
import numpy as np

D_MODEL = 2048
BATCH = 8
SEQ = 2048
DEPTH = 1
DEC_BATCH = 128
DEC_SEQ = 4
PAST_LEN = 16384
PAGE_SIZE = 128

RWKV_WIDTH = D_MODEL // 2
RWKV_HEAD = 64
RWKV_HEADS = RWKV_WIDTH // RWKV_HEAD
DECAY_LORA = 96
AAA_LORA = 96
GATE_LORA = 256
MLA_V = 128
MLA_HEADS = (D_MODEL - RWKV_WIDTH) // MLA_V
MLA_WIDTH = MLA_HEADS * MLA_V
MLA_NOPE = 128
MLA_ROPE = 64
Q_RANK = 512
KV_RANK = 256
D_FF = 4 * D_MODEL
ROPE_THETA = 10000.0
Q_BLOCK = 128
N_SHIFT = 3 * RWKV_WIDTH + DECAY_LORA + AAA_LORA + GATE_LORA
N_IN = N_SHIFT + Q_RANK + KV_RANK + MLA_ROPE
DEEPNORM_ALPHA = (2.0 * DEPTH) ** 0.25
DEEPNORM_BETA = (8.0 * DEPTH) ** -0.25
LN_EPS = 1e-5
RMS_EPS = 1e-6
GN_EPS = 64e-5

kernel_name = "hymba_rwkv7_mla_deepnorm_adaln_step"

F32 = jnp.float32


def _layernorm(x, g, b):
    xf = x.astype(F32)
    mu = jnp.mean(xf, -1, keepdims=True)
    var = jnp.mean(jnp.square(xf - mu), -1, keepdims=True)
    return ((xf - mu) * lax.rsqrt(var + LN_EPS) * g + b).astype(x.dtype)


def _rmsnorm(x, g):
    xf = x.astype(F32)
    return (xf * lax.rsqrt(jnp.mean(xf * xf, -1, keepdims=True) + RMS_EPS) * g).astype(x.dtype)


def _rope(x, pos):
    half = MLA_ROPE // 2
    inv = ROPE_THETA ** (-jnp.arange(half, dtype=F32) / half)
    ang = pos.astype(F32)[:, None] * inv
    ang = ang.reshape(ang.shape[:1] + (1,) * (x.ndim - 3) + (half,))
    cos, sin = jnp.cos(ang), jnp.sin(ang)
    xf = x.astype(F32)
    x1, x2 = xf[..., :half], xf[..., half:]
    return jnp.concatenate([x1 * cos - x2 * sin, x1 * sin + x2 * cos], -1).astype(x.dtype)


def _latent_attend(q_lat, q_rope, lat, kr, q_pos, k_pos):
    scale = (MLA_NOPE + MLA_ROPE) ** -0.5
    latf = lat.astype(F32)
    s = (jnp.einsum('qhr,kr->hqk', q_lat.astype(F32), latf)
         + jnp.einsum('qhd,kd->hqk', q_rope.astype(F32), kr.astype(F32))) * scale
    s = jnp.where(k_pos[None, None, :] <= q_pos[None, :, None], s, -jnp.inf)
    p = jax.nn.softmax(s, axis=-1)
    return jnp.einsum('hqk,kr->qhr', p, latf)


def _prompt_attention(q_lat, q_rope, lat, kr):
    B, T = q_lat.shape[:2]
    nb = T // Q_BLOCK
    pos = jnp.arange(T)

    def blk(args):
        ql, qr, qp = args
        return jax.vmap(_latent_attend, in_axes=(0, 0, 0, 0, None, None))(ql, qr, lat, kr, qp, pos)

    split = lambda a: a.reshape((B, nb, Q_BLOCK) + a.shape[2:]).swapaxes(0, 1)
    out = lax.map(blk, (split(q_lat), split(q_rope), pos.reshape(nb, Q_BLOCK)))
    return out.swapaxes(0, 1).reshape((B, T) + out.shape[3:])


def _sample_attention(q_lat, q_rope, lat, kr, cache_lat, cache_kr, page_table):
    T = q_lat.shape[1]
    past = page_table.shape[1] * cache_lat.shape[1]
    q_pos = past + jnp.arange(T)
    k_pos = jnp.arange(past + T)

    def one(args):
        ql, qr, ln, kn, pt = args
        lat_all = jnp.concatenate([cache_lat[pt].reshape(past, -1), ln.astype(cache_lat.dtype)], 0)
        kr_all = jnp.concatenate([cache_kr[pt].reshape(past, -1), kn.astype(cache_kr.dtype)], 0)
        return _latent_attend(ql, qr, lat_all, kr_all, q_pos, k_pos)

    return lax.map(one, (q_lat, q_rope, lat, kr, page_table))


def _wkv_scan(S0, r, decay, k, v, kk, a):
    def step(S, inp):
        r_t, w_t, k_t, v_t, kk_t, a_t = inp
        sa = jnp.einsum('bhvk,bhk->bhv', S, kk_t)
        S = (S * w_t[:, :, None, :] - sa[..., None] * (kk_t * a_t)[:, :, None, :]
             + v_t[..., None] * k_t[:, :, None, :])
        return S, jnp.einsum('bhvk,bhk->bhv', S, r_t)

    tm = lambda t: jnp.moveaxis(t, 1, 0)
    S, o = lax.scan(step, S0, (tm(r), tm(decay), tm(k), tm(v), tm(kk), tm(a)))
    return S, jnp.moveaxis(o, 0, 1)


def _rwkv_group(p_rwkv, prev, S0, p):
    B, T, _ = p_rwkv.shape
    C = RWKV_WIDTH
    shifted = jnp.concatenate([prev[:, None].astype(p_rwkv.dtype), p_rwkv[:, :-1]], 1)
    z = (p_rwkv + p['mu'] * (shifted - p_rwkv)).astype(F32)
    r, k, v = z[..., :C], z[..., C:2 * C], z[..., 2 * C:3 * C]
    o0 = 3 * C
    wd = z[..., o0:o0 + DECAY_LORA]
    ad = z[..., o0 + DECAY_LORA:o0 + DECAY_LORA + AAA_LORA]
    gd = z[..., o0 + DECAY_LORA + AAA_LORA:]
    w_log = -jax.nn.softplus(-(p['w0'] + jnp.tanh(wd) @ p['w2'])) - 0.5
    decay = jnp.exp(-jnp.exp(w_log))
    a = jax.nn.sigmoid(p['a0'] + ad @ p['a2'])
    g = jax.nn.sigmoid(gd) @ p['g2']
    hs = lambda t: t.reshape(B, T, RWKV_HEADS, RWKV_HEAD)
    kk = hs(k * p['k_k'])
    kk = kk / jnp.maximum(jnp.sqrt(jnp.sum(kk * kk, -1, keepdims=True)), 1e-12)
    k = k * (1.0 + (a - 1.0) * p['k_a'])
    r, k, v, a, decay = hs(r), hs(k), hs(v), hs(a), hs(decay)
    S, o = _wkv_scan(S0.astype(F32), r, decay, k, v, kk, a)
    mu = jnp.mean(o, -1, keepdims=True)
    var = jnp.mean(jnp.square(o - mu), -1, keepdims=True)
    o = ((o - mu) * lax.rsqrt(var + GN_EPS)).reshape(B, T, C) * p['lnx_g'] + p['lnx_b']
    o = o + (jnp.sum(r * k * p['r_k'], -1, keepdims=True) * v).reshape(B, T, C)
    return o * g, S, p_rwkv[:, -1]


def _mla_project(p_mla, pos, p):
    B, T, _ = p_mla.shape
    cq = p_mla[..., :Q_RANK]
    ckv = p_mla[..., Q_RANK:Q_RANK + KV_RANK]
    kr = p_mla[..., Q_RANK + KV_RANK:]
    q = (_rmsnorm(cq, p['q_norm']) @ p['w_q_up']).reshape(B, T, MLA_HEADS, MLA_NOPE + MLA_ROPE)
    q_rope = _rope(q[..., MLA_NOPE:], pos)
    q_lat = jnp.einsum('bthn,rhn->bthr', q[..., :MLA_NOPE], p['w_uk'])
    lat = _rmsnorm(ckv, p['kv_norm'])
    return q_lat, q_rope, lat, _rope(kr, pos)


def _layer(x, c, pos, S0, prev, attend, p):
    mod = jax.nn.silu(c) @ p['w_ada'] + p['b_ada']
    sh1, sc1, g1, sh2, sc2, g2 = jnp.split(mod[:, None, :], 6, axis=-1)
    h = x * (1.0 + sc1) + sh1
    proj = h @ p['w_in']
    o_r, S, last = _rwkv_group(proj[..., :N_SHIFT], prev, S0, p)
    q_lat, q_rope, lat, kr = _mla_project(proj[..., N_SHIFT:], pos, p)
    ctx = attend(q_lat, q_rope, lat, kr)
    o_m = jnp.einsum('bthr,rhv->bthv', ctx, p['w_uv']).reshape(x.shape[0], x.shape[1], MLA_WIDTH)
    o_m = _rmsnorm(o_m, p['out_norm'])
    mixed = jnp.concatenate([o_r.astype(x.dtype), o_m.astype(x.dtype)], -1) @ p['w_out']
    x = _layernorm(DEEPNORM_ALPHA * x + g1 * mixed, p['ln1_g'], p['ln1_b'])
    h2 = x * (1.0 + sc2) + sh2
    f = jnp.square(jax.nn.relu(h2 @ p['w_up'])) @ p['w_down']
    x = _layernorm(DEEPNORM_ALPHA * x + g2 * f, p['ln2_g'], p['ln2_b'])
    return x, lat, kr, S, last


def setup_inputs(seed: int = 0) -> dict:
    key = jax.random.key(seed)
    ks = iter(jax.random.split(key, 48))
    nrm = lambda shape, s=1.0: s * jax.random.normal(next(ks), shape, F32)
    one = lambda shape: 1.0 + nrm(shape, 0.05)
    L = DEPTH
    n_pages = PAST_LEN // PAGE_SIZE
    n_pool = (DEC_BATCH * n_pages * 5) // 4
    page_table = jax.random.permutation(next(ks), n_pool)[:DEC_BATCH * n_pages]
    page_table = page_table.reshape(DEC_BATCH, n_pages).astype(jnp.int32)
    inp = {}
    inp['x_prompt'] = nrm((BATCH, SEQ, D_MODEL))
    inp['x_sample'] = nrm((DEC_BATCH, DEC_SEQ, D_MODEL))
    inp['c_prompt'] = nrm((BATCH, D_MODEL))
    inp['c_sample'] = nrm((DEC_BATCH, D_MODEL))
    inp['cache_latent'] = nrm((L, n_pool, PAGE_SIZE, KV_RANK))
    inp['cache_krope'] = nrm((L, n_pool, PAGE_SIZE, MLA_ROPE))
    inp['state_wkv'] = nrm((L, DEC_BATCH, RWKV_HEADS, RWKV_HEAD, RWKV_HEAD), 0.3)
    inp['state_shift'] = nrm((L, DEC_BATCH, N_SHIFT))
    inp['page_table'] = page_table
    inp['w_ada'] = nrm((L, D_MODEL, 6 * D_MODEL), D_MODEL ** -0.5)
    inp['b_ada'] = nrm((L, 6 * D_MODEL), 0.02)
    inp['w_in'] = nrm((L, D_MODEL, N_IN), D_MODEL ** -0.5)
    inp['rwkv_mu'] = jax.random.uniform(next(ks), (L, N_SHIFT), F32)
    inp['rwkv_w0'] = jax.random.uniform(next(ks), (L, RWKV_WIDTH), F32, -4.0, 1.0)
    inp['rwkv_w2'] = nrm((L, DECAY_LORA, RWKV_WIDTH), DECAY_LORA ** -0.5)
    inp['rwkv_a0'] = nrm((L, RWKV_WIDTH), 0.1)
    inp['rwkv_a2'] = nrm((L, AAA_LORA, RWKV_WIDTH), AAA_LORA ** -0.5)
    inp['rwkv_g2'] = nrm((L, GATE_LORA, RWKV_WIDTH), GATE_LORA ** -0.5)
    inp['rwkv_k_k'] = 0.85 + nrm((L, RWKV_WIDTH), 0.05)
    inp['rwkv_k_a'] = one((L, RWKV_WIDTH))
    inp['rwkv_r_k'] = nrm((L, RWKV_HEADS, RWKV_HEAD), 0.1)
    inp['rwkv_lnx_g'] = one((L, RWKV_WIDTH))
    inp['rwkv_lnx_b'] = nrm((L, RWKV_WIDTH), 0.02)
    inp['mla_q_norm'] = one((L, Q_RANK))
    inp['mla_w_q_up'] = nrm((L, Q_RANK, MLA_HEADS * (MLA_NOPE + MLA_ROPE)), Q_RANK ** -0.5)
    inp['mla_kv_norm'] = one((L, KV_RANK))
    inp['mla_w_uk'] = nrm((L, KV_RANK, MLA_HEADS, MLA_NOPE), KV_RANK ** -0.5)
    inp['mla_w_uv'] = nrm((L, KV_RANK, MLA_HEADS, MLA_V), KV_RANK ** -0.5)
    inp['mla_out_norm'] = one((L, MLA_WIDTH))
    inp['w_out'] = nrm((L, D_MODEL, D_MODEL), DEEPNORM_BETA * D_MODEL ** -0.5)
    inp['ln1_g'] = one((L, D_MODEL))
    inp['ln1_b'] = nrm((L, D_MODEL), 0.02)
    inp['w_up'] = nrm((L, D_MODEL, D_FF), D_MODEL ** -0.5)
    inp['w_down'] = nrm((L, D_FF, D_MODEL), DEEPNORM_BETA * D_FF ** -0.5)
    inp['ln2_g'] = one((L, D_MODEL))
    inp['ln2_b'] = nrm((L, D_MODEL), 0.02)
    return inp


def reference(x_prompt, x_sample, c_prompt, c_sample, cache_latent, cache_krope, state_wkv,
              state_shift, page_table, w_ada, b_ada, w_in, rwkv_mu, rwkv_w0, rwkv_w2, rwkv_a0,
              rwkv_a2, rwkv_g2, rwkv_k_k, rwkv_k_a, rwkv_r_k, rwkv_lnx_g, rwkv_lnx_b, mla_q_norm,
              mla_w_q_up, mla_kv_norm, mla_w_uk, mla_w_uv, mla_out_norm, w_out, ln1_g, ln1_b,
              w_up, w_down, ln2_g, ln2_b):
    B, T = x_prompt.shape[:2]
    Td = x_sample.shape[1]
    past = page_table.shape[1] * cache_latent.shape[2]
    pos_p = jnp.arange(T)
    pos_s = past + jnp.arange(Td)
    yp, ys = x_prompt, x_sample
    lat_p, kr_p, wkv_p, sh_p = [], [], [], []
    lat_s, kr_s, wkv_s, sh_s = [], [], [], []
    for l in range(DEPTH):
        p = dict(w_ada=w_ada[l], b_ada=b_ada[l], w_in=w_in[l], mu=rwkv_mu[l], w0=rwkv_w0[l],
                 w2=rwkv_w2[l], a0=rwkv_a0[l], a2=rwkv_a2[l], g2=rwkv_g2[l], k_k=rwkv_k_k[l],
                 k_a=rwkv_k_a[l], r_k=rwkv_r_k[l], lnx_g=rwkv_lnx_g[l], lnx_b=rwkv_lnx_b[l],
                 q_norm=mla_q_norm[l], w_q_up=mla_w_q_up[l], kv_norm=mla_kv_norm[l],
                 w_uk=mla_w_uk[l], w_uv=mla_w_uv[l], out_norm=mla_out_norm[l], w_out=w_out[l],
                 ln1_g=ln1_g[l], ln1_b=ln1_b[l], w_up=w_up[l], w_down=w_down[l],
                 ln2_g=ln2_g[l], ln2_b=ln2_b[l])
        S0 = jnp.zeros((B, RWKV_HEADS, RWKV_HEAD, RWKV_HEAD), F32)
        prev0 = jnp.zeros((B, N_SHIFT), x_prompt.dtype)
        yp, lp, kp, Sp, hp = _layer(yp, c_prompt, pos_p, S0, prev0, _prompt_attention, p)
        att_s = lambda ql, qr, ln, kn, cl=cache_latent[l], ck=cache_krope[l]: _sample_attention(
            ql, qr, ln, kn, cl, ck, page_table)
        ys, ls, kss, Ss, hs_ = _layer(ys, c_sample, pos_s, state_wkv[l], state_shift[l], att_s, p)
        lat_p.append(lp); kr_p.append(kp); wkv_p.append(Sp.astype(x_prompt.dtype)); sh_p.append(hp)
        lat_s.append(ls); kr_s.append(kss); wkv_s.append(Ss.astype(state_wkv.dtype)); sh_s.append(hs_)
    return (yp, ys, jnp.stack(lat_p), jnp.stack(kr_p), jnp.stack(wkv_p), jnp.stack(sh_p),
            jnp.stack(lat_s), jnp.stack(kr_s), jnp.stack(wkv_s), jnp.stack(sh_s))
```

```python
import functools

import jax
import jax.numpy as jnp
from jax import lax
from jax.experimental import pallas as pl
from jax.experimental.pallas import tpu as pltpu

F32 = jnp.float32
BF16 = jnp.bfloat16

D_MODEL = 2048
RWKV_WIDTH = 1024
RWKV_HEAD = 64
RWKV_HEADS = 16
DECAY_LORA = 96
AAA_LORA = 96
GATE_LORA = 256
MLA_V = 128
MLA_HEADS = 8
MLA_WIDTH = 1024
MLA_NOPE = 128
MLA_ROPE = 64
Q_RANK = 512
KV_RANK = 256
D_FF = 4 * D_MODEL
ROPE_THETA = 10000.0
N_SHIFT = 3 * RWKV_WIDTH + DECAY_LORA + AAA_LORA + GATE_LORA
N_IN = N_SHIFT + Q_RANK + KV_RANK + MLA_ROPE
DEPTH = 1
DEEPNORM_ALPHA = (2.0 * DEPTH) ** 0.25
LN_EPS = 1e-5
RMS_EPS = 1e-6
GN_EPS = 64e-5
ATTN_SCALE = (MLA_NOPE + MLA_ROPE) ** -0.5
NEG_BIG = -1e30

LANES = 128
VMEM_LIMIT = 56 * 1024 * 1024

OFF_R, OFF_K, OFF_V = 0, RWKV_WIDTH, 2 * RWKV_WIDTH
OFF_GD = 3 * RWKV_WIDTH
OFF_WA = OFF_GD + GATE_LORA
RW_COLS = OFF_WA + 256
OFF_CQ = RW_COLS
OFF_CKV = OFF_CQ + Q_RANK
OFF_KR = OFF_CKV + KV_RANK
NP_COLS = 4608
MLA_BLK = 512


def _perm_cols(a):
    z = lambda n: jnp.zeros(a.shape[:-1] + (n,), a.dtype)
    o_wd = 3 * RWKV_WIDTH
    o_gd = o_wd + DECAY_LORA + AAA_LORA
    pieces = [a[..., :o_wd], a[..., o_gd:N_SHIFT], a[..., o_wd:o_gd], z(256 - DECAY_LORA - AAA_LORA)]
    if a.shape[-1] == N_IN:
        pieces += [a[..., N_SHIFT:N_IN], z(NP_COLS - OFF_KR - MLA_ROPE)]
    return jnp.concatenate(pieces, -1)


def _unperm_shift_cols(a):
    return jnp.concatenate([a[..., :OFF_GD], a[..., OFF_WA:OFF_WA + DECAY_LORA + AAA_LORA],
                            a[..., OFF_GD:OFF_WA]], -1)


def _params(**kw):
    return pltpu.CompilerParams(vmem_limit_bytes=VMEM_LIMIT, **kw)


def _sigmoid(x):
    return 1.0 / (1.0 + jnp.exp(-x))


def _layernorm(x, g, b):
    mu = jnp.mean(x, -1, keepdims=True)
    xc = x - mu
    var = jnp.mean(xc * xc, -1, keepdims=True)
    return xc * lax.rsqrt(var + LN_EPS) * g + b


def _rmsnorm(x, g):
    return x * lax.rsqrt(jnp.mean(x * x, -1, keepdims=True) + RMS_EPS) * g


def _ada_body(c_ref, w_ref, b_ref, o_ref):
    c = c_ref[...]
    a = (c * _sigmoid(c)).astype(BF16)
    o_ref[...] = jnp.dot(a, w_ref[...].astype(BF16), preferred_element_type=F32) + b_ref[...]


def _ada(c, w_ada, b_ada):
    rows, d = c.shape
    n = w_ada.shape[1]
    tn = 1024
    return pl.pallas_call(
        _ada_body,
        grid=(n // tn,),
        in_specs=[pl.BlockSpec((rows, d), lambda j: (0, 0)),
                  pl.BlockSpec((d, tn), lambda j: (0, j)),
                  pl.BlockSpec((1, tn), lambda j: (0, j))],
        out_specs=pl.BlockSpec((rows, tn), lambda j: (0, j)),
        out_shape=jax.ShapeDtypeStruct((rows, n), F32),
        compiler_params=_params(dimension_semantics=("parallel",)),
        name="ada_mod",
    )(c, w_ada, b_ada.reshape(1, n))


def _grouped_spec(tm, d, col=0):
    return pl.BlockSpec((None, tm, d), lambda b, r, *_: (b, r, col))


def _tmajor_spec(tm, d, width, col=0):
    per_b = width // d
    return pl.BlockSpec((tm, d), lambda b, r, *_: (r, b * per_b + col))


def _mod_spec(mod, tm, chunk):
    if mod.shape[1] == 1:
        return pl.BlockSpec((None, 1, D_MODEL), lambda b, r, *_: (b, 0, chunk))
    return pl.BlockSpec((None, tm, D_MODEL), lambda b, r, *_: (b, r, chunk))


def _const_spec(shape):
    nd = len(shape)
    return pl.BlockSpec(shape, lambda *_: (0,) * nd)


def _inproj_body(x_ref, sc_ref, sh_ref, w_ref, o_ref, h_scr):
    @pl.when(pl.program_id(2) == 0)
    def _():
        h = x_ref[...] * (1.0 + sc_ref[...]) + sh_ref[...]
        h_scr[...] = h.astype(BF16)

    o_ref[...] = jnp.dot(h_scr[...], w_ref[...], preferred_element_type=F32)


def _in_proj(x, mod, w_in_p):
    B, T, D = x.shape
    tm = min(512, T)
    nj = 2
    tn = NP_COLS // nj
    return pl.pallas_call(
        _inproj_body,
        grid=(B, T // tm, nj),
        in_specs=[_grouped_spec(tm, D), _mod_spec(mod, tm, 1), _mod_spec(mod, tm, 0),
                  pl.BlockSpec((D, tn), lambda b, r, j: (0, j))],
        out_specs=pl.BlockSpec((tm, tn), lambda b, r, j: (r, b * nj + j)),
        out_shape=jax.ShapeDtypeStruct((T, B * NP_COLS), F32),
        scratch_shapes=[pltpu.VMEM((tm, D), BF16)],
        compiler_params=_params(dimension_semantics=("parallel", "parallel", "arbitrary")),
        name="in_proj",
    )(x, mod, mod, w_in_p).reshape(T, B, NP_COLS)


def _rwkv_pre_body(p_ref, pb_ref, prev_ref, mu_ref, wa0_ref, wl_ref, g2_ref,
                   r_ref, k_ref, v_ref, w_ref, a_ref, g_ref):
    tt, nb, _ = p_ref.shape
    p = p_ref[...]
    first = jnp.where(pl.program_id(0) == 0, prev_ref[...], pb_ref[0])
    shifted = jnp.concatenate([first[None], p[:-1]], axis=0)
    z = (p + mu_ref[...] * (shifted - p)).reshape(tt * nb, RW_COLS)
    out3 = lambda t: t.reshape(tt, nb, RWKV_WIDTH)
    r_ref[...] = out3(z[:, OFF_R:OFF_R + RWKV_WIDTH])
    k_ref[...] = out3(z[:, OFF_K:OFF_K + RWKV_WIDTH])
    v_ref[...] = out3(z[:, OFF_V:OFF_V + RWKV_WIDTH])
    wa = z[:, OFF_WA:OFF_WA + 256]
    lane = lax.broadcasted_iota(jnp.int32, wa.shape, 1)
    lora_in = jnp.where(lane < DECAY_LORA, jnp.tanh(wa), wa).astype(BF16)
    pre = jnp.dot(lora_in, wl_ref[...], preferred_element_type=F32) + wa0_ref[...]
    u = pre[:, :RWKV_WIDTH]
    w_log = jnp.minimum(u, 0.0) - jnp.log1p(jnp.exp(-jnp.abs(u))) - 0.5
    w_ref[...] = out3(jnp.exp(-jnp.exp(w_log)))
    a_ref[...] = out3(_sigmoid(pre[:, RWKV_WIDTH:]))
    gd = _sigmoid(z[:, OFF_GD:OFF_GD + GATE_LORA]).astype(BF16)
    g_ref[...] = out3(jnp.dot(gd, g2_ref[...], preferred_element_type=F32))


def _rwkv_pre(proj, prev_p, mu_p, wa0, w_lora, g2):
    T, B, _ = proj.shape
    tt = max(min(256 // B, T), 1)
    blk = pl.BlockSpec((tt, B, RWKV_WIDTH), lambda i: (i, 0, 0))
    out = jax.ShapeDtypeStruct((T, B, RWKV_WIDTH), F32)
    return pl.pallas_call(
        _rwkv_pre_body,
        grid=(T // tt,),
        in_specs=[pl.BlockSpec((tt, B, RW_COLS), lambda i: (i, 0, 0)),
                  pl.BlockSpec((1, B, RW_COLS), lambda i: (jnp.maximum(i * tt - 1, 0), 0, 0)),
                  _const_spec((B, RW_COLS)), _const_spec((1, RW_COLS)), _const_spec((1, 2 * RWKV_WIDTH)),
                  _const_spec((256, 2 * RWKV_WIDTH)), _const_spec((GATE_LORA, RWKV_WIDTH))],
        out_specs=[blk] * 6,
        out_shape=[out] * 6,
        compiler_params=_params(dimension_semantics=("parallel",)),
        name="rwkv_pre",
    )(proj, proj, prev_p, mu_p, wa0, w_lora, g2)


def _wkv_body(r_ref, k_ref, v_ref, w_ref, a_ref, g_ref, kk_p, ka_p, rk_p, lg_p, lb_p, s0_ref,
              o_ref, sout_ref, S, b_kk, b_wr, b_w, b_kka, b_k2):
    tt = r_ref.shape[0]
    N = RWKV_HEAD

    @pl.when(pl.program_id(1) == 0)
    def _():
        S[...] = s0_ref[...]

    def tok(t, carry):
        rT = r_ref[t].T
        kT = k_ref[t].T
        vT = v_ref[t].T
        wT = w_ref[t].T
        aT = a_ref[t].T
        kk = kT * kk_p[...]
        nrm = jnp.sqrt(jnp.sum(kk * kk, axis=0, keepdims=True))
        kk = kk / jnp.maximum(nrm, 1e-12)
        k2 = kT * (1.0 + (aT - 1.0) * ka_p[...])
        kka = kk * aT
        b_kk[...] = kk
        b_wr[...] = wT * rT
        b_w[...] = wT
        b_kka[...] = kka
        b_k2[...] = k2

        sa = jnp.zeros((N, LANES), F32)
        o1 = jnp.zeros((N, LANES), F32)
        for j in range(N):
            Sj = S[j]
            sa = sa + Sj * b_kk[j:j + 1, :]
            o1 = o1 + Sj * b_wr[j:j + 1, :]
        c1 = jnp.sum(kka * rT, axis=0, keepdims=True)
        c2 = jnp.sum(k2 * rT, axis=0, keepdims=True)
        o = o1 - sa * c1 + vT * c2
        for j in range(N):
            S[j] = S[j] * b_w[j:j + 1, :] - sa * b_kka[j:j + 1, :] + vT * b_k2[j:j + 1, :]

        mu = jnp.mean(o, axis=0, keepdims=True)
        oc = o - mu
        var = jnp.mean(oc * oc, axis=0, keepdims=True)
        on = oc * lax.rsqrt(var + GN_EPS) * lg_p[...] + lb_p[...]
        bonus = jnp.sum(rT * k2 * rk_p[...], axis=0, keepdims=True)
        on = on + bonus * vT
        o_ref[t] = on.T * g_ref[t]
        return carry

    lax.fori_loop(0, tt, tok, 0)

    @pl.when(pl.program_id(1) == pl.num_programs(1) - 1)
    def _():
        sout_ref[...] = S[...]


def _wkv(r, k, v, w, a, g, tiles, s0):
    T, CH, N = r.shape
    G = CH // LANES
    tt = min(16, T)
    tok_spec = pl.BlockSpec((tt, LANES, N), lambda g_, i: (i, g_, 0))
    st_spec = pl.BlockSpec((None, N, N, LANES), lambda g_, i: (g_, 0, 0, 0))
    tile_spec = pl.BlockSpec((N, LANES), lambda g_, i: (0, 0))
    return pl.pallas_call(
        _wkv_body,
        grid=(G, T // tt),
        in_specs=[tok_spec] * 6 + [tile_spec] * 5 + [st_spec],
        out_specs=[tok_spec, st_spec],
        out_shape=[jax.ShapeDtypeStruct((T, CH, N), F32), jax.ShapeDtypeStruct((G, N, N, LANES), F32)],
        scratch_shapes=[pltpu.VMEM((N, N, LANES), F32)] + [pltpu.VMEM((N, LANES), F32)] * 5,
        compiler_params=_params(dimension_semantics=("parallel", "arbitrary")),
        name="wkv_scan",
    )(r, k, v, w, a, g, *tiles, s0)


def _rot_half(x):
    n = x.shape[-1]
    lane = lax.broadcasted_iota(jnp.int32, x.shape, x.ndim - 1)
    fwd = pltpu.roll(x, n - MLA_ROPE // 2, axis=x.ndim - 1)
    bwd = pltpu.roll(x, MLA_ROPE // 2, axis=x.ndim - 1)
    return jnp.where(lane % MLA_ROPE < MLA_ROPE // 2, fwd, bwd)


def _mla_body(cq_ref, ckr_ref, cos_ref, sin_ref, qn_ref, wq_ref, wuk_ref, kvn_ref,
              q_ref, kc_ref, lat_ref, kr_ref):
    cq = cq_ref[...]
    q = jnp.dot(_rmsnorm(cq, qn_ref[...]).astype(BF16), wq_ref[...], preferred_element_type=F32)
    nope_w = MLA_HEADS * MLA_NOPE
    qr = q[:, nope_w:]
    q_rope = qr * cos_ref[...] + _rot_half(qr) * sin_ref[...]
    for h in range(MLA_HEADS):
        qn = q[:, h * MLA_NOPE:(h + 1) * MLA_NOPE].astype(BF16)
        q_ref[h, :, :KV_RANK] = jnp.dot(qn, wuk_ref[h], preferred_element_type=F32).astype(BF16)
        q_ref[h, :, KV_RANK:] = q_rope[:, h * MLA_ROPE:(h + 1) * MLA_ROPE].astype(BF16)
    ckr = ckr_ref[...]
    lat = _rmsnorm(ckr[:, :KV_RANK], kvn_ref[...])
    slab = ckr[:, KV_RANK:KV_RANK + LANES]
    kr = (slab * cos_ref[:, :LANES] + _rot_half(slab) * sin_ref[:, :LANES])[:, :MLA_ROPE]
    lat_ref[...] = lat
    kr_ref[...] = kr
    kc_ref[:, :KV_RANK] = lat.astype(BF16)
    kc_ref[:, KV_RANK:] = kr.astype(BF16)


def _mla_proj(proj, cos_t, sin_t, q_norm, wq_p, wuk_t, kv_norm):
    T, B, _ = proj.shape
    tm = min(512, T)
    qk = KV_RANK + MLA_ROPE
    proj2 = proj.reshape(T, B * NP_COLS)
    tab = pl.BlockSpec((None, tm, MLA_HEADS * MLA_ROPE), lambda b, r: (0, r, 0))
    return pl.pallas_call(
        _mla_body,
        grid=(B, T // tm),
        in_specs=[_tmajor_spec(tm, MLA_BLK, NP_COLS, OFF_CQ // MLA_BLK),
                  _tmajor_spec(tm, MLA_BLK, NP_COLS, OFF_CKV // MLA_BLK),
                  tab, tab, _const_spec((1, Q_RANK)), _const_spec(wq_p.shape), _const_spec(wuk_t.shape),
                  _const_spec((1, KV_RANK))],
        out_specs=[pl.BlockSpec((None, MLA_HEADS, tm, qk), lambda b, r: (b, 0, r, 0)),
                   _grouped_spec(tm, qk), _grouped_spec(tm, KV_RANK), _grouped_spec(tm, MLA_ROPE)],
        out_shape=[jax.ShapeDtypeStruct((B, MLA_HEADS, T, qk), BF16),
                   jax.ShapeDtypeStruct((B, T, qk), BF16),
                   jax.ShapeDtypeStruct((B, T, KV_RANK), F32),
                   jax.ShapeDtypeStruct((B, T, MLA_ROPE), F32)],
        compiler_params=_params(dimension_semantics=("parallel", "parallel")),
        name="mla_proj",
    )(proj2, proj2, cos_t, sin_t, q_norm, wq_p, wuk_t, kv_norm)


def _pattn_body(q_ref, k_ref, o_ref, m_scr, l_scr, acc_scr, *, tq, tk):
    i = pl.program_id(1)
    j = pl.program_id(2)
    rows = MLA_HEADS * tq
    last_j = ((i + 1) * tq - 1) // tk

    @pl.when(j == 0)
    def _():
        m_scr[...] = jnp.full(m_scr.shape, NEG_BIG, F32)
        l_scr[...] = jnp.zeros(l_scr.shape, F32)
        acc_scr[...] = jnp.zeros(acc_scr.shape, F32)

    @pl.when(j <= last_j)
    def _():
        q = q_ref[...].reshape(rows, q_ref.shape[-1])
        k = k_ref[...]
        s = lax.dot_general(q, k, (((1,), (1,)), ((), ())), preferred_element_type=F32) * ATTN_SCALE
        qpos = i * tq + lax.broadcasted_iota(jnp.int32, (MLA_HEADS, tq, tk), 1).reshape(rows, tk)
        kpos = j * tk + lax.broadcasted_iota(jnp.int32, (rows, tk), 1)
        s = jnp.where(kpos <= qpos, s, NEG_BIG)
        m_prev = m_scr[...]
        m_new = jnp.maximum(m_prev, jnp.max(s, axis=1, keepdims=True))
        alpha = jnp.exp(m_prev - m_new)
        p = jnp.exp(s - m_new)
        l_scr[...] = alpha * l_scr[...] + jnp.sum(p, axis=1, keepdims=True)
        acc_scr[...] = alpha * acc_scr[...] + jnp.dot(p.astype(BF16), k[:, :KV_RANK],
                                                      preferred_element_type=F32)
        m_scr[...] = m_new

    @pl.when(j == pl.num_programs(2) - 1)
    def _():
        ctx = acc_scr[...] / l_scr[...]
        o_ref[...] = ctx.reshape(MLA_HEADS, tq, KV_RANK).astype(BF16)


def _prompt_attn(q4, kc):
    B, H, T, qk = q4.shape
    tq = min(256, T)
    tk = min(512, T)
    body = functools.partial(_pattn_body, tq=tq, tk=tk)
    rows = H * tq
    return pl.pallas_call(
        body,
        grid=(B, T // tq, T // tk),
        in_specs=[pl.BlockSpec((None, H, tq, qk), lambda b, i, j: (b, 0, i, 0)),
                  pl.BlockSpec((None, tk, qk),
                               lambda b, i, j: (b, jnp.minimum(j, ((i + 1) * tq - 1) // tk), 0))],
        out_specs=pl.BlockSpec((None, H, tq, KV_RANK), lambda b, i, j: (b, 0, i, 0)),
        out_shape=jax.ShapeDtypeStruct((B, H, T, KV_RANK), BF16),
        scratch_shapes=[pltpu.VMEM((rows, 1), F32), pltpu.VMEM((rows, 1), F32),
                        pltpu.VMEM((rows, KV_RANK), F32)],
        compiler_params=_params(dimension_semantics=("parallel", "parallel", "arbitrary")),
        name="prompt_attn",
    )(q4, kc)


def _sattn_body(pt_ref, q_ref, latn_ref, krn_ref, *rest, n_pg, n_new):
    lat_refs = rest[:n_pg]
    kr_refs = rest[n_pg:2 * n_pg]
    o_ref, m_scr, l_scr, acc_scr = rest[2 * n_pg:]
    c = pl.program_id(1)
    nt = (((1,), (1,)), ((), ()))

    @pl.when(c == 0)
    def _():
        m_scr[...] = jnp.full(m_scr.shape, NEG_BIG, F32)
        l_scr[...] = jnp.zeros(l_scr.shape, F32)
        acc_scr[...] = jnp.zeros(acc_scr.shape, F32)

    q = q_ref[...]
    q_lat = q[:, :KV_RANK]
    q_rope = q[:, KV_RANK:]

    def update(s, vals):
        m_prev = m_scr[...]
        m_new = jnp.maximum(m_prev, jnp.max(s, axis=1, keepdims=True))
        alpha = jnp.exp(m_prev - m_new)
        p = jnp.exp(s - m_new)
        l_scr[...] = alpha * l_scr[...] + jnp.sum(p, axis=1, keepdims=True)
        acc_scr[...] = alpha * acc_scr[...] + jnp.dot(p.astype(BF16), vals, preferred_element_type=F32)
        m_scr[...] = m_new

    lats = [lat_refs[p][...].astype(BF16) for p in range(n_pg)]
    s_parts = []
    for p in range(n_pg):
        sp = lax.dot_general(q_lat, lats[p], nt, preferred_element_type=F32)
        sp = sp + lax.dot_general(q_rope, kr_refs[p][...].astype(BF16), nt, preferred_element_type=F32)
        s_parts.append(sp)
    s = jnp.concatenate(s_parts, axis=1) * ATTN_SCALE
    update(s, jnp.concatenate(lats, axis=0))

    @pl.when(c == pl.num_programs(1) - 1)
    def _():
        latn = latn_ref[...].astype(BF16)
        sn = lax.dot_general(q_lat, latn, nt, preferred_element_type=F32)
        sn = (sn + lax.dot_general(q_rope, krn_ref[...].astype(BF16), nt, preferred_element_type=F32)) * ATTN_SCALE
        row_t = lax.broadcasted_iota(jnp.int32, sn.shape, 0) % n_new
        col_t = lax.broadcasted_iota(jnp.int32, sn.shape, 1)
        sn = jnp.where(col_t <= row_t, sn, NEG_BIG)
        update(sn, latn)
        o_ref[...] = acc_scr[...] / l_scr[...]


def _sample_attn(q, latn, krn, cache_lat, cache_kr, page_table):
    S, rows, qk = q.shape
    n_pages = page_table.shape[1]
    page = cache_lat.shape[2]
    n_pg = min(16, n_pages)
    n_new = rows // MLA_HEADS
    pad_new = latn.shape[1]
    lat_specs = [pl.BlockSpec((None, None, page, KV_RANK),
                              lambda s, c, pt, p=p: (0, pt[s, c * n_pg + p], 0, 0)) for p in range(n_pg)]
    kr_specs = [pl.BlockSpec((None, None, page, MLA_ROPE),
                             lambda s, c, pt, p=p: (0, pt[s, c * n_pg + p], 0, 0)) for p in range(n_pg)]
    grid_spec = pltpu.PrefetchScalarGridSpec(
        num_scalar_prefetch=1,
        grid=(S, n_pages // n_pg),
        in_specs=[pl.BlockSpec((None, rows, qk), lambda s, c, pt: (s, 0, 0)),
                  pl.BlockSpec((None, pad_new, KV_RANK), lambda s, c, pt: (s, 0, 0)),
                  pl.BlockSpec((None, pad_new, MLA_ROPE), lambda s, c, pt: (s, 0, 0))] + lat_specs + kr_specs,
        out_specs=pl.BlockSpec((None, rows, KV_RANK), lambda s, c, pt: (s, 0, 0)),
        scratch_shapes=[pltpu.VMEM((rows, 1), F32), pltpu.VMEM((rows, 1), F32),
                        pltpu.VMEM((rows, KV_RANK), F32)],
    )
    return pl.pallas_call(
        functools.partial(_sattn_body, n_pg=n_pg, n_new=n_new),
        grid_spec=grid_spec,
        out_shape=jax.ShapeDtypeStruct((S, rows, KV_RANK), F32),
        compiler_params=_params(dimension_semantics=("parallel", "arbitrary")),
        name="sample_attn",
    )(page_table, q, latn, krn, *([cache_lat] * n_pg), *([cache_kr] * n_pg))


def _uv_body(ctx_ref, wuv_ref, on_ref, o_ref):
    parts = [jnp.dot(ctx_ref[h].astype(BF16), wuv_ref[h], preferred_element_type=F32)
             for h in range(MLA_HEADS)]
    om = jnp.concatenate(parts, axis=1)
    o_ref[...] = _rmsnorm(om, on_ref[...]).astype(BF16)


def _uv_norm(ctx, wuv_t, out_norm):
    B, H, T, R = ctx.shape
    tm = min(512, T)
    return pl.pallas_call(
        _uv_body,
        grid=(B, T // tm),
        in_specs=[pl.BlockSpec((None, H, tm, R), lambda b, r: (b, 0, r, 0)),
                  _const_spec(wuv_t.shape), _const_spec((1, MLA_WIDTH))],
        out_specs=_grouped_spec(tm, MLA_WIDTH),
        out_shape=jax.ShapeDtypeStruct((B, T, MLA_WIDTH), BF16),
        compiler_params=_params(dimension_semantics=("parallel", "parallel")),
        name="uv_norm",
    )(ctx, wuv_t, out_norm)


def _outproj_body(or_ref, om_ref, x_ref, g1_ref, w_ref, lg_ref, lb_ref, o_ref):
    mixed = jnp.dot(or_ref[...].astype(BF16), w_ref[:RWKV_WIDTH, :], preferred_element_type=F32)
    mixed = mixed + jnp.dot(om_ref[...], w_ref[RWKV_WIDTH:, :], preferred_element_type=F32)
    y = DEEPNORM_ALPHA * x_ref[...] + g1_ref[...] * mixed
    o_ref[...] = _layernorm(y, lg_ref[...], lb_ref[...])


def _out_proj(o_r, o_m, x, mod, w_out, ln_g, ln_b):
    B, T, D = x.shape
    tm = min(256, T)
    return pl.pallas_call(
        _outproj_body,
        grid=(B, T // tm),
        in_specs=[_tmajor_spec(tm, RWKV_WIDTH, RWKV_WIDTH), _grouped_spec(tm, MLA_WIDTH), _grouped_spec(tm, D),
                  _mod_spec(mod, tm, 2), _const_spec((D, D)), _const_spec((1, D)), _const_spec((1, D))],
        out_specs=_grouped_spec(tm, D),
        out_shape=jax.ShapeDtypeStruct((B, T, D), F32),
        compiler_params=_params(dimension_semantics=("parallel", "parallel")),
        name="out_proj_ln1",
    )(o_r.reshape(T, B * RWKV_WIDTH), o_m, x, mod, w_out, ln_g, ln_b)


def _ffn_body(x_ref, sc_ref, sh_ref, g2_ref, wu_ref, wd_ref, lg_ref, lb_ref, o_ref, h_scr, acc_scr):
    f = pl.program_id(2)

    @pl.when(f == 0)
    def _():
        h_scr[...] = (x_ref[...] * (1.0 + sc_ref[...]) + sh_ref[...]).astype(BF16)
        acc_scr[...] = jnp.zeros(acc_scr.shape, F32)

    u = jnp.maximum(jnp.dot(h_scr[...], wu_ref[...], preferred_element_type=F32), 0.0)
    acc_scr[...] += jnp.dot((u * u).astype(BF16), wd_ref[...], preferred_element_type=F32)

    @pl.when(f == pl.num_programs(2) - 1)
    def _():
        y = DEEPNORM_ALPHA * x_ref[...] + g2_ref[...] * acc_scr[...]
        o_ref[...] = _layernorm(y, lg_ref[...], lb_ref[...])


def _ffn(x, mod, w_up, w_down, ln_g, ln_b):
    B, T, D = x.shape
    tm = min(512, T)
    tf = 1024
    return pl.pallas_call(
        _ffn_body,
        grid=(B, T // tm, D_FF // tf),
        in_specs=[_grouped_spec(tm, D), _mod_spec(mod, tm, 4), _mod_spec(mod, tm, 3), _mod_spec(mod, tm, 5),
                  pl.BlockSpec((D, tf), lambda b, r, f: (0, f)),
                  pl.BlockSpec((tf, D), lambda b, r, f: (f, 0)),
                  _const_spec((1, D)), _const_spec((1, D))],
        out_specs=_grouped_spec(tm, D),
        out_shape=jax.ShapeDtypeStruct((B, T, D), F32),
        scratch_shapes=[pltpu.VMEM((tm, D), BF16), pltpu.VMEM((tm, D), F32)],
        compiler_params=_params(dimension_semantics=("parallel", "parallel", "arbitrary")),
        name="ffn_ln2",
    )(x, mod, mod, mod, w_up, w_down, ln_g, ln_b)


def _rope_tables(pos):
    half = MLA_ROPE // 2
    inv = ROPE_THETA ** (-jnp.arange(half, dtype=F32) / half)
    ang = pos.astype(F32)[:, None] * inv
    cos, sin = jnp.cos(ang), jnp.sin(ang)
    cos_t = jnp.tile(jnp.concatenate([cos, cos], -1), (1, MLA_HEADS))
    sin_t = jnp.tile(jnp.concatenate([-sin, sin], -1), (1, MLA_HEADS))
    return cos_t[None], sin_t[None]


def _head_tile(p):
    t = p.reshape(RWKV_HEADS, RWKV_HEAD).T
    return jnp.tile(t, (1, LANES // RWKV_HEADS))


def _layer_front(x, mod, pos, prev_p, s0, wp):
    B, T, _ = x.shape
    proj = _in_proj(x, mod, wp['w_in'])
    r, k, v, w, a, g = _rwkv_pre(proj, prev_p, wp['mu'], wp['wa0'], wp['w_lora'], wp['g2'])
    ch = lambda t: t.reshape(T, B * RWKV_HEADS, RWKV_HEAD)
    o_r, s_out = _wkv(ch(r), ch(k), ch(v), ch(w), ch(a), ch(g), wp['tiles'], s0)
    o_r = o_r.reshape(T, B, RWKV_WIDTH)
    cos_t, sin_t = _rope_tables(pos)
    q4, kc, lat, kr = _mla_proj(proj, cos_t, sin_t, wp['q_norm'], wp['wq'], wp['wuk'], wp['kv_norm'])
    last = _unperm_shift_cols(proj[T - 1, :, :RW_COLS])
    return o_r, s_out, q4, kc, lat, kr, last


def _layer_back(x, mod, o_r, ctx, wp):
    o_m = _uv_norm(ctx, wp['wuv'], wp['out_norm'])
    x1 = _out_proj(o_r, o_m, x, mod, wp['w_out'], wp['ln1_g'], wp['ln1_b'])
    return _ffn(x1, mod, wp['w_up'], wp['w_down'], wp['ln2_g'], wp['ln2_b'])


def _state_to_lanes(s):
    ch = s.shape[0]
    return s.reshape(ch // LANES, LANES, RWKV_HEAD, RWKV_HEAD).transpose(0, 3, 2, 1)


def _state_from_lanes(s):
    g = s.shape[0]
    return s.transpose(0, 3, 2, 1).reshape(g * LANES, RWKV_HEAD, RWKV_HEAD)


def kernel(x_prompt, x_sample, c_prompt, c_sample, cache_latent, cache_krope, state_wkv, state_shift, page_table, w_ada, b_ada, w_in, rwkv_mu, rwkv_w0, rwkv_w2, rwkv_a0, rwkv_a2, rwkv_g2, rwkv_k_k, rwkv_k_a, rwkv_r_k, rwkv_lnx_g, rwkv_lnx_b, mla_q_norm, mla_w_q_up, mla_kv_norm, mla_w_uk, mla_w_uv, mla_out_norm, w_out, ln1_g, ln1_b, w_up, w_down, ln2_g, ln2_b):
    B, T, D = x_prompt.shape
    S, Td, _ = x_sample.shape
    past = page_table.shape[1] * cache_latent.shape[2]
    l = 0

    w_lora = jnp.zeros((256, 2 * RWKV_WIDTH), F32)
    w_lora = w_lora.at[:DECAY_LORA, :RWKV_WIDTH].set(rwkv_w2[l])
    w_lora = w_lora.at[DECAY_LORA:DECAY_LORA + AAA_LORA, RWKV_WIDTH:].set(rwkv_a2[l])
    wq = mla_w_q_up[l].reshape(Q_RANK, MLA_HEADS, MLA_NOPE + MLA_ROPE)
    wq_p = jnp.concatenate([wq[:, :, :MLA_NOPE].reshape(Q_RANK, -1), wq[:, :, MLA_NOPE:].reshape(Q_RANK, -1)], -1)
    row = lambda p: p.reshape(1, -1)
    wp = dict(
        w_in=_perm_cols(w_in[l]).astype(BF16),
        mu=row(_perm_cols(rwkv_mu[l])),
        wa0=row(jnp.concatenate([rwkv_w0[l], rwkv_a0[l]])),
        w_lora=w_lora.astype(BF16),
        g2=rwkv_g2[l].astype(BF16),
        tiles=[_head_tile(rwkv_k_k[l]), _head_tile(rwkv_k_a[l]), _head_tile(rwkv_r_k[l].reshape(-1)),
               _head_tile(rwkv_lnx_g[l]), _head_tile(rwkv_lnx_b[l])],
        q_norm=row(mla_q_norm[l]), wq=wq_p.astype(BF16),
        wuk=mla_w_uk[l].transpose(1, 2, 0).astype(BF16),
        kv_norm=row(mla_kv_norm[l]),
        wuv=mla_w_uv[l].transpose(1, 0, 2).astype(BF16),
        out_norm=row(mla_out_norm[l]),
        w_out=w_out[l].astype(BF16), ln1_g=row(ln1_g[l]), ln1_b=row(ln1_b[l]),
        w_up=w_up[l].astype(BF16), w_down=w_down[l].astype(BF16), ln2_g=row(ln2_g[l]), ln2_b=row(ln2_b[l]),
    )

    mod = _ada(jnp.concatenate([c_prompt, c_sample], 0), w_ada[l], b_ada[l])
    mod_p = mod[:B, None, :]
    mod_s = jnp.tile(mod[B:], (Td, 1))[None]

    s0_p = jnp.zeros((B * RWKV_HEADS // LANES, RWKV_HEAD, RWKV_HEAD, LANES), F32)
    prev0 = jnp.zeros((B, RW_COLS), F32)
    o_r, s_p, q4, kc, lat_p, kr_p, last_p = _layer_front(x_prompt, mod_p, jnp.arange(T), prev0, s0_p, wp)
    ctx_p = _prompt_attn(q4, kc)
    y_p = _layer_back(x_prompt, mod_p, o_r, ctx_p, wp)
    wkv_p = _state_from_lanes(s_p).reshape(B, RWKV_HEADS, RWKV_HEAD, RWKV_HEAD)

    xs = x_sample.transpose(1, 0, 2).reshape(1, Td * S, D)
    pos_s = jnp.repeat(past + jnp.arange(Td), S)
    s0_s = _state_to_lanes(state_wkv[l].reshape(S * RWKV_HEADS, RWKV_HEAD, RWKV_HEAD))
    prev_s = _perm_cols(state_shift[l])
    proj = _in_proj(xs, mod_s, wp['w_in']).reshape(Td, S, NP_COLS)
    r, k, v, w, a, g = _rwkv_pre(proj, prev_s, wp['mu'], wp['wa0'], wp['w_lora'], wp['g2'])
    ch = lambda t: t.reshape(Td, S * RWKV_HEADS, RWKV_HEAD)
    o_rs, s_s = _wkv(ch(r), ch(k), ch(v), ch(w), ch(a), ch(g), wp['tiles'], s0_s)
    o_rs = o_rs.reshape(Td * S, 1, RWKV_WIDTH)
    cos_t, sin_t = _rope_tables(pos_s)
    q4s, _, lat_s, kr_s = _mla_proj(proj.reshape(Td * S, 1, NP_COLS), cos_t, sin_t, wp['q_norm'], wp['wq'],
                                    wp['wuk'], wp['kv_norm'])
    last_s = _unperm_shift_cols(proj[Td - 1, :, :RW_COLS])
    qk = KV_RANK + MLA_ROPE
    q_s = q4s.reshape(MLA_HEADS, Td, S, qk).transpose(2, 0, 1, 3).reshape(S, MLA_HEADS * Td, qk)
    lat_s = lat_s.reshape(Td, S, KV_RANK).transpose(1, 0, 2)
    kr_s = kr_s.reshape(Td, S, MLA_ROPE).transpose(1, 0, 2)
    pad16 = lambda t: jnp.pad(t, ((0, 0), (0, 16 - Td), (0, 0)))
    ctx_s = _sample_attn(q_s, pad16(lat_s), pad16(kr_s), cache_latent[l:l + 1], cache_krope[l:l + 1], page_table)
    ctx_s = ctx_s.reshape(S, MLA_HEADS, Td, KV_RANK).transpose(1, 2, 0, 3).reshape(1, MLA_HEADS, Td * S, KV_RANK)
    y_s = _layer_back(xs, mod_s, o_rs, ctx_s, wp)
    y_s = y_s.reshape(Td, S, D).transpose(1, 0, 2)
    wkv_s = _state_from_lanes(s_s).reshape(S, RWKV_HEADS, RWKV_HEAD, RWKV_HEAD)

    return (y_p, y_s, lat_p[None], kr_p[None], wkv_p[None], last_p[None],
            lat_s[None], kr_s[None], wkv_s[None], last_s[None])
```

```python
import functools

import jax
import jax.numpy as jnp
from jax import lax
from jax.experimental import pallas as pl
from jax.experimental.pallas import tpu as pltpu

F32 = jnp.float32
BF16 = jnp.bfloat16

D_MODEL = 2048
RWKV_WIDTH = 1024
RWKV_HEAD = 64
RWKV_HEADS = 16
DECAY_LORA = 96
AAA_LORA = 96
GATE_LORA = 256
MLA_V = 128
MLA_HEADS = 8
MLA_WIDTH = 1024
MLA_NOPE = 128
MLA_ROPE = 64
Q_RANK = 512
KV_RANK = 256
D_FF = 4 * D_MODEL
ROPE_THETA = 10000.0
N_SHIFT = 3 * RWKV_WIDTH + DECAY_LORA + AAA_LORA + GATE_LORA
N_IN = N_SHIFT + Q_RANK + KV_RANK + MLA_ROPE
DEPTH = 1
DEEPNORM_ALPHA = (2.0 * DEPTH) ** 0.25
LN_EPS = 1e-5
RMS_EPS = 1e-6
GN_EPS = 64e-5
ATTN_SCALE = (MLA_NOPE + MLA_ROPE) ** -0.5
NEG_BIG = -1e30

LANES = 128
VMEM_LIMIT = 56 * 1024 * 1024

OFF_R, OFF_K, OFF_V = 0, RWKV_WIDTH, 2 * RWKV_WIDTH
OFF_GD = 3 * RWKV_WIDTH
OFF_WA = OFF_GD + GATE_LORA
RW_COLS = OFF_WA + 256
OFF_CQ = RW_COLS
OFF_CKV = OFF_CQ + Q_RANK
OFF_KR = OFF_CKV + KV_RANK
NP_COLS = 4608
MLA_BLK = 512


def _perm_cols(a):
    z = lambda n: jnp.zeros(a.shape[:-1] + (n,), a.dtype)
    o_wd = 3 * RWKV_WIDTH
    o_gd = o_wd + DECAY_LORA + AAA_LORA
    pieces = [a[..., :o_wd], a[..., o_gd:N_SHIFT], a[..., o_wd:o_gd], z(256 - DECAY_LORA - AAA_LORA)]
    if a.shape[-1] == N_IN:
        pieces += [a[..., N_SHIFT:N_IN], z(NP_COLS - OFF_KR - MLA_ROPE)]
    return jnp.concatenate(pieces, -1)


def _unperm_shift_cols(a):
    return jnp.concatenate([a[..., :OFF_GD], a[..., OFF_WA:OFF_WA + DECAY_LORA + AAA_LORA],
                            a[..., OFF_GD:OFF_WA]], -1)


def _params(**kw):
    return pltpu.CompilerParams(vmem_limit_bytes=VMEM_LIMIT, **kw)


def _sigmoid(x):
    return 1.0 / (1.0 + jnp.exp(-x))


def _layernorm(x, g, b):
    mu = jnp.mean(x, -1, keepdims=True)
    xc = x - mu
    var = jnp.mean(xc * xc, -1, keepdims=True)
    return xc * lax.rsqrt(var + LN_EPS) * g + b


def _rmsnorm(x, g):
    return x * lax.rsqrt(jnp.mean(x * x, -1, keepdims=True) + RMS_EPS) * g


def _ada_body(c_ref, w_ref, b_ref, o_ref):
    c = c_ref[...]
    a = (c * _sigmoid(c)).astype(BF16)
    o_ref[...] = jnp.dot(a, w_ref[...].astype(BF16), preferred_element_type=F32) + b_ref[...]


def _ada(c, w_ada, b_ada):
    rows, d = c.shape
    n = w_ada.shape[1]
    tn = 1024
    return pl.pallas_call(
        _ada_body,
        grid=(n // tn,),
        in_specs=[pl.BlockSpec((rows, d), lambda j: (0, 0)),
                  pl.BlockSpec((d, tn), lambda j: (0, j)),
                  pl.BlockSpec((1, tn), lambda j: (0, j))],
        out_specs=pl.BlockSpec((rows, tn), lambda j: (0, j)),
        out_shape=jax.ShapeDtypeStruct((rows, n), F32),
        compiler_params=_params(dimension_semantics=("parallel",)),
        name="ada_mod",
    )(c, w_ada, b_ada.reshape(1, n))


def _grouped_spec(tm, d, col=0):
    return pl.BlockSpec((None, tm, d), lambda b, r, *_: (b, r, col))


def _tmajor_spec(tm, d, width, col=0):
    per_b = width // d
    return pl.BlockSpec((tm, d), lambda b, r, *_: (r, b * per_b + col))


def _mod_spec(mod, tm, chunk):
    if mod.shape[1] == 1:
        return pl.BlockSpec((None, 1, D_MODEL), lambda b, r, *_: (b, 0, chunk))
    return pl.BlockSpec((None, tm, D_MODEL), lambda b, r, *_: (b, r, chunk))


def _const_spec(shape):
    nd = len(shape)
    return pl.BlockSpec(shape, lambda *_: (0,) * nd)


def _inproj_body(x_ref, sc_ref, sh_ref, w_ref, o_ref, h_scr):
    @pl.when(pl.program_id(2) == 0)
    def _():
        h = x_ref[...] * (1.0 + sc_ref[...]) + sh_ref[...]
        h_scr[...] = h.astype(BF16)

    o_ref[...] = jnp.dot(h_scr[...], w_ref[...], preferred_element_type=F32)


def _in_proj(x, mod, w_in_p):
    B, T, D = x.shape
    tm = min(512, T)
    nj = 2
    tn = NP_COLS // nj
    return pl.pallas_call(
        _inproj_body,
        grid=(B, T // tm, nj),
        in_specs=[_grouped_spec(tm, D), _mod_spec(mod, tm, 1), _mod_spec(mod, tm, 0),
                  pl.BlockSpec((D, tn), lambda b, r, j: (0, j))],
        out_specs=pl.BlockSpec((tm, tn), lambda b, r, j: (r, b * nj + j)),
        out_shape=jax.ShapeDtypeStruct((T, B * NP_COLS), F32),
        scratch_shapes=[pltpu.VMEM((tm, D), BF16)],
        compiler_params=_params(dimension_semantics=("parallel", "parallel", "arbitrary")),
        name="in_proj",
    )(x, mod, mod, w_in_p).reshape(T, B, NP_COLS)


def _rwkv_pre_body(p_ref, pb_ref, prev_ref, mu_ref, wa0_ref, wl_ref, g2_ref,
                   r_ref, k_ref, v_ref, w_ref, a_ref, g_ref):
    tt, nb, _ = p_ref.shape
    p = p_ref[...]
    first = jnp.where(pl.program_id(0) == 0, prev_ref[...], pb_ref[0])
    shifted = jnp.concatenate([first[None], p[:-1]], axis=0)
    z = (p + mu_ref[...] * (shifted - p)).reshape(tt * nb, RW_COLS)
    out3 = lambda t: t.reshape(tt, nb, RWKV_WIDTH)
    r_ref[...] = out3(z[:, OFF_R:OFF_R + RWKV_WIDTH])
    k_ref[...] = out3(z[:, OFF_K:OFF_K + RWKV_WIDTH])
    v_ref[...] = out3(z[:, OFF_V:OFF_V + RWKV_WIDTH])
    wa = z[:, OFF_WA:OFF_WA + 256]
    lane = lax.broadcasted_iota(jnp.int32, wa.shape, 1)
    lora_in = jnp.where(lane < DECAY_LORA, jnp.tanh(wa), wa).astype(BF16)
    pre = jnp.dot(lora_in, wl_ref[...], preferred_element_type=F32) + wa0_ref[...]
    u = pre[:, :RWKV_WIDTH]
    w_log = jnp.minimum(u, 0.0) - jnp.log1p(jnp.exp(-jnp.abs(u))) - 0.5
    w_ref[...] = out3(jnp.exp(-jnp.exp(w_log)))
    a_ref[...] = out3(_sigmoid(pre[:, RWKV_WIDTH:]))
    gd = _sigmoid(z[:, OFF_GD:OFF_GD + GATE_LORA]).astype(BF16)
    g_ref[...] = out3(jnp.dot(gd, g2_ref[...], preferred_element_type=F32))


def _rwkv_pre(proj, prev_p, mu_p, wa0, w_lora, g2):
    T, B, _ = proj.shape
    tt = max(min(256 // B, T), 1)
    blk = pl.BlockSpec((tt, B, RWKV_WIDTH), lambda i: (i, 0, 0))
    out = jax.ShapeDtypeStruct((T, B, RWKV_WIDTH), F32)
    return pl.pallas_call(
        _rwkv_pre_body,
        grid=(T // tt,),
        in_specs=[pl.BlockSpec((tt, B, RW_COLS), lambda i: (i, 0, 0)),
                  pl.BlockSpec((1, B, RW_COLS), lambda i: (jnp.maximum(i * tt - 1, 0), 0, 0)),
                  _const_spec((B, RW_COLS)), _const_spec((1, RW_COLS)), _const_spec((1, 2 * RWKV_WIDTH)),
                  _const_spec((256, 2 * RWKV_WIDTH)), _const_spec((GATE_LORA, RWKV_WIDTH))],
        out_specs=[blk] * 6,
        out_shape=[out] * 6,
        compiler_params=_params(dimension_semantics=("parallel",)),
        name="rwkv_pre",
    )(proj, proj, prev_p, mu_p, wa0, w_lora, g2)


WKV_UNROLL = 8


def _wkv_body(r_ref, k_ref, v_ref, w_ref, a_ref, kk_p, ka_p, rk_p, lg_p, lb_p, s0_ref,
              o_ref, sout_ref, S, b_kk, b_wr, b_w, b_kka, b_k2):
    tt = r_ref.shape[0]
    N = RWKV_HEAD

    @pl.when(pl.program_id(1) == 0)
    def _():
        S[...] = s0_ref[...]

    def tok(t, carry):
        rT = r_ref[t]
        kT = k_ref[t]
        aT = a_ref[t]
        wT = w_ref[t]
        kk = kT * kk_p[...]
        nrm = jnp.sqrt(jnp.sum(kk * kk, axis=0, keepdims=True))
        kk = kk / jnp.maximum(nrm, 1e-12)
        k2 = kT * (1.0 + (aT - 1.0) * ka_p[...])
        kka = kk * aT
        b_kk[...] = kk
        b_wr[...] = wT * rT
        b_w[...] = wT
        b_kka[...] = kka
        b_k2[...] = k2
        c1 = jnp.sum(kka * rT, axis=0, keepdims=True)
        c2 = jnp.sum(k2 * rT, axis=0, keepdims=True)
        bonus = jnp.sum(rT * k2 * rk_p[...], axis=0, keepdims=True)

        def reduce_pass(j0, acc):
            sa, o1 = acc
            for jj in range(WKV_UNROLL):
                j = j0 * WKV_UNROLL + jj
                Sj = S[j]
                sa = sa + Sj * b_kk[pl.ds(j, 1), :]
                o1 = o1 + Sj * b_wr[pl.ds(j, 1), :]
            return sa, o1

        zero = jnp.zeros((N, LANES), F32)
        sa, o1 = lax.fori_loop(0, N // WKV_UNROLL, reduce_pass, (zero, zero))
        vT = v_ref[t]
        o = o1 - sa * c1 + vT * c2

        def update_pass(j0, c):
            for jj in range(WKV_UNROLL):
                j = j0 * WKV_UNROLL + jj
                S[j] = S[j] * b_w[pl.ds(j, 1), :] - sa * b_kka[pl.ds(j, 1), :] + vT * b_k2[pl.ds(j, 1), :]
            return c

        lax.fori_loop(0, N // WKV_UNROLL, update_pass, 0)

        mu = jnp.mean(o, axis=0, keepdims=True)
        oc = o - mu
        var = jnp.mean(oc * oc, axis=0, keepdims=True)
        on = oc * lax.rsqrt(var + GN_EPS) * lg_p[...] + lb_p[...]
        o_ref[t] = on + bonus * vT
        return carry

    lax.fori_loop(0, tt, tok, 0)

    @pl.when(pl.program_id(1) == pl.num_programs(1) - 1)
    def _():
        sout_ref[...] = S[...]


def _wkv(r, k, v, w, a, tiles, s0):
    T, G, N, _ = r.shape
    tt = min(32, T)
    tok_spec = pl.BlockSpec((tt, None, N, LANES), lambda g_, i: (i, g_, 0, 0))
    st_spec = pl.BlockSpec((None, N, N, LANES), lambda g_, i: (g_, 0, 0, 0))
    per_group = tiles[0].shape[0] > 1
    tile_spec = pl.BlockSpec((None, N, LANES), lambda g_, i: (g_ if per_group else 0, 0, 0))
    return pl.pallas_call(
        _wkv_body,
        grid=(G, T // tt),
        in_specs=[tok_spec] * 5 + [tile_spec] * 5 + [st_spec],
        out_specs=[tok_spec, st_spec],
        out_shape=[jax.ShapeDtypeStruct((T, G, N, LANES), F32), jax.ShapeDtypeStruct((G, N, N, LANES), F32)],
        scratch_shapes=[pltpu.VMEM((N, N, LANES), F32)] + [pltpu.VMEM((N, LANES), F32)] * 5,
        compiler_params=_params(dimension_semantics=("parallel", "arbitrary")),
        name="wkv_scan",
    )(r, k, v, w, a, *tiles, s0)


def _rot_half(x):
    n = x.shape[-1]
    lane = lax.broadcasted_iota(jnp.int32, x.shape, x.ndim - 1)
    fwd = pltpu.roll(x, n - MLA_ROPE // 2, axis=x.ndim - 1)
    bwd = pltpu.roll(x, MLA_ROPE // 2, axis=x.ndim - 1)
    return jnp.where(lane % MLA_ROPE < MLA_ROPE // 2, fwd, bwd)


def _mla_body(cq_ref, ckr_ref, cos_ref, sin_ref, qn_ref, wq_ref, wuk_ref, kvn_ref,
              q_ref, kc_ref, lat_ref, kr_ref):
    cq = cq_ref[...]
    q = jnp.dot(_rmsnorm(cq, qn_ref[...]).astype(BF16), wq_ref[...], preferred_element_type=F32)
    nope_w = MLA_HEADS * MLA_NOPE
    qr = q[:, nope_w:]
    q_rope = qr * cos_ref[...] + _rot_half(qr) * sin_ref[...]
    for h in range(MLA_HEADS):
        qn = q[:, h * MLA_NOPE:(h + 1) * MLA_NOPE].astype(BF16)
        q_ref[h, :, :KV_RANK] = jnp.dot(qn, wuk_ref[h], preferred_element_type=F32).astype(BF16)
        q_ref[h, :, KV_RANK:] = q_rope[:, h * MLA_ROPE:(h + 1) * MLA_ROPE].astype(BF16)
    ckr = ckr_ref[...]
    lat = _rmsnorm(ckr[:, :KV_RANK], kvn_ref[...])
    slab = ckr[:, KV_RANK:KV_RANK + LANES]
    kr = (slab * cos_ref[:, :LANES] + _rot_half(slab) * sin_ref[:, :LANES])[:, :MLA_ROPE]
    lat_ref[...] = lat
    kr_ref[...] = kr
    kc_ref[:, :KV_RANK] = lat.astype(BF16)
    kc_ref[:, KV_RANK:] = kr.astype(BF16)


def _mla_proj(proj, cos_t, sin_t, q_norm, wq_p, wuk_t, kv_norm):
    T, B, _ = proj.shape
    tm = min(512, T)
    qk = KV_RANK + MLA_ROPE
    proj2 = proj.reshape(T, B * NP_COLS)
    tab = pl.BlockSpec((None, tm, MLA_HEADS * MLA_ROPE), lambda b, r: (0, r, 0))
    return pl.pallas_call(
        _mla_body,
        grid=(B, T // tm),
        in_specs=[_tmajor_spec(tm, MLA_BLK, NP_COLS, OFF_CQ // MLA_BLK),
                  _tmajor_spec(tm, MLA_BLK, NP_COLS, OFF_CKV // MLA_BLK),
                  tab, tab, _const_spec((1, Q_RANK)), _const_spec(wq_p.shape), _const_spec(wuk_t.shape),
                  _const_spec((1, KV_RANK))],
        out_specs=[pl.BlockSpec((None, MLA_HEADS, tm, qk), lambda b, r: (b, 0, r, 0)),
                   _grouped_spec(tm, qk), _grouped_spec(tm, KV_RANK), _grouped_spec(tm, MLA_ROPE)],
        out_shape=[jax.ShapeDtypeStruct((B, MLA_HEADS, T, qk), BF16),
                   jax.ShapeDtypeStruct((B, T, qk), BF16),
                   jax.ShapeDtypeStruct((B, T, KV_RANK), F32),
                   jax.ShapeDtypeStruct((B, T, MLA_ROPE), F32)],
        compiler_params=_params(dimension_semantics=("parallel", "parallel")),
        name="mla_proj",
    )(proj2, proj2, cos_t, sin_t, q_norm, wq_p, wuk_t, kv_norm)


def _pattn_body(q_ref, k_ref, o_ref, m_scr, l_scr, acc_scr, *, tq, tk):
    i = pl.program_id(1)
    j = pl.program_id(2)
    rows = MLA_HEADS * tq
    last_j = ((i + 1) * tq - 1) // tk

    @pl.when(j == 0)
    def _():
        m_scr[...] = jnp.full(m_scr.shape, NEG_BIG, F32)
        l_scr[...] = jnp.zeros(l_scr.shape, F32)
        acc_scr[...] = jnp.zeros(acc_scr.shape, F32)

    def step(masked):
        q = q_ref[...].reshape(rows, q_ref.shape[-1])
        k = k_ref[...]
        s = lax.dot_general(q, k, (((1,), (1,)), ((), ())), preferred_element_type=F32) * ATTN_SCALE
        if masked:
            qpos = i * tq + lax.broadcasted_iota(jnp.int32, (MLA_HEADS, tq, tk), 1).reshape(rows, tk)
            kpos = j * tk + lax.broadcasted_iota(jnp.int32, (rows, tk), 1)
            s = jnp.where(kpos <= qpos, s, NEG_BIG)
        m_prev = m_scr[...]
        m_new = jnp.maximum(m_prev, jnp.max(s, axis=1, keepdims=True))
        alpha = jnp.exp(m_prev - m_new)
        p = jnp.exp(s - m_new)
        l_scr[...] = alpha * l_scr[...] + jnp.sum(p, axis=1, keepdims=True)
        acc_scr[...] = alpha * acc_scr[...] + jnp.dot(p.astype(BF16), k[:, :KV_RANK],
                                                      preferred_element_type=F32)
        m_scr[...] = m_new

    crosses_diagonal = (j + 1) * tk - 1 > i * tq
    pl.when(jnp.logical_and(j <= last_j, crosses_diagonal))(functools.partial(step, True))
    pl.when(jnp.logical_and(j <= last_j, jnp.logical_not(crosses_diagonal)))(functools.partial(step, False))

    @pl.when(j == pl.num_programs(2) - 1)
    def _():
        ctx = acc_scr[...] / l_scr[...]
        o_ref[...] = ctx.reshape(MLA_HEADS, tq, KV_RANK).astype(BF16)


def _prompt_attn(q4, kc):
    B, H, T, qk = q4.shape
    tq = min(256, T)
    tk = min(512, T)
    body = functools.partial(_pattn_body, tq=tq, tk=tk)
    rows = H * tq
    return pl.pallas_call(
        body,
        grid=(B, T // tq, T // tk),
        in_specs=[pl.BlockSpec((None, H, tq, qk), lambda b, i, j: (b, 0, i, 0)),
                  pl.BlockSpec((None, tk, qk),
                               lambda b, i, j: (b, jnp.minimum(j, ((i + 1) * tq - 1) // tk), 0))],
        out_specs=pl.BlockSpec((None, H, tq, KV_RANK), lambda b, i, j: (b, 0, i, 0)),
        out_shape=jax.ShapeDtypeStruct((B, H, T, KV_RANK), BF16),
        scratch_shapes=[pltpu.VMEM((rows, 1), F32), pltpu.VMEM((rows, 1), F32),
                        pltpu.VMEM((rows, KV_RANK), F32)],
        compiler_params=_params(dimension_semantics=("parallel", "parallel", "arbitrary")),
        name="prompt_attn",
    )(q4, kc)


def _sattn_body(pt_ref, q_ref, latn_ref, krn_ref, *rest, n_pg, n_new):
    lat_refs = rest[:n_pg]
    kr_refs = rest[n_pg:2 * n_pg]
    o_ref, m_scr, l_scr, acc_scr = rest[2 * n_pg:]
    c = pl.program_id(1)
    nt = (((1,), (1,)), ((), ()))

    @pl.when(c == 0)
    def _():
        m_scr[...] = jnp.full(m_scr.shape, NEG_BIG, F32)
        l_scr[...] = jnp.zeros(l_scr.shape, F32)
        acc_scr[...] = jnp.zeros(acc_scr.shape, F32)

    q = q_ref[...]
    q_lat = q[:, :KV_RANK]
    q_rope = q[:, KV_RANK:]

    def update(s, vals):
        m_prev = m_scr[...]
        m_new = jnp.maximum(m_prev, jnp.max(s, axis=1, keepdims=True))
        alpha = jnp.exp(m_prev - m_new)
        p = jnp.exp(s - m_new)
        l_scr[...] = alpha * l_scr[...] + jnp.sum(p, axis=1, keepdims=True)
        acc_scr[...] = alpha * acc_scr[...] + jnp.dot(p.astype(BF16), vals, preferred_element_type=F32)
        m_scr[...] = m_new

    lat_all = jnp.concatenate([lat_refs[p][...].astype(BF16) for p in range(n_pg)], axis=0)
    krt_all = jnp.concatenate([kr_refs[p][...].astype(BF16) for p in range(n_pg)], axis=1)
    s = lax.dot_general(q_lat, lat_all, nt, preferred_element_type=F32)
    s = (s + jnp.dot(q_rope, krt_all, preferred_element_type=F32)) * ATTN_SCALE
    update(s, lat_all)

    @pl.when(c == pl.num_programs(1) - 1)
    def _():
        latn = latn_ref[...].astype(BF16)
        sn = lax.dot_general(q_lat, latn, nt, preferred_element_type=F32)
        sn = (sn + lax.dot_general(q_rope, krn_ref[...].astype(BF16), nt, preferred_element_type=F32)) * ATTN_SCALE
        row_t = lax.broadcasted_iota(jnp.int32, sn.shape, 0) % n_new
        col_t = lax.broadcasted_iota(jnp.int32, sn.shape, 1)
        sn = jnp.where(col_t <= row_t, sn, NEG_BIG)
        update(sn, latn)
        o_ref[...] = acc_scr[...] / l_scr[...]


def _sample_attn(q, latn, krn, cache_lat, cache_kr, page_table):
    S, rows, qk = q.shape
    n_pages = page_table.shape[1]
    page = cache_lat.shape[1]
    n_pg = min(16, n_pages)
    n_new = rows // MLA_HEADS
    pad_new = latn.shape[1]
    lat_specs = [pl.BlockSpec((None, page, KV_RANK),
                              lambda s, c, pt, p=p: (pt[s, c * n_pg + p], 0, 0)) for p in range(n_pg)]
    kr_specs = [pl.BlockSpec((None, MLA_ROPE, page),
                             lambda s, c, pt, p=p: (pt[s, c * n_pg + p], 0, 0)) for p in range(n_pg)]
    grid_spec = pltpu.PrefetchScalarGridSpec(
        num_scalar_prefetch=1,
        grid=(S, n_pages // n_pg),
        in_specs=[pl.BlockSpec((None, rows, qk), lambda s, c, pt: (s, 0, 0)),
                  pl.BlockSpec((None, pad_new, KV_RANK), lambda s, c, pt: (s, 0, 0)),
                  pl.BlockSpec((None, pad_new, MLA_ROPE), lambda s, c, pt: (s, 0, 0))] + lat_specs + kr_specs,
        out_specs=pl.BlockSpec((None, rows, KV_RANK), lambda s, c, pt: (s, 0, 0)),
        scratch_shapes=[pltpu.VMEM((rows, 1), F32), pltpu.VMEM((rows, 1), F32),
                        pltpu.VMEM((rows, KV_RANK), F32)],
    )
    return pl.pallas_call(
        functools.partial(_sattn_body, n_pg=n_pg, n_new=n_new),
        grid_spec=grid_spec,
        out_shape=jax.ShapeDtypeStruct((S, rows, KV_RANK), F32),
        compiler_params=_params(dimension_semantics=("parallel", "arbitrary")),
        name="sample_attn",
    )(page_table, q, latn, krn, *([cache_lat] * n_pg), *([cache_kr] * n_pg))


def _uv_body(ctx_ref, wuv_ref, on_ref, o_ref):
    parts = [jnp.dot(ctx_ref[h].astype(BF16), wuv_ref[h], preferred_element_type=F32)
             for h in range(MLA_HEADS)]
    om = jnp.concatenate(parts, axis=1)
    o_ref[...] = _rmsnorm(om, on_ref[...]).astype(BF16)


def _uv_norm(ctx, wuv_t, out_norm):
    B, H, T, R = ctx.shape
    tm = min(512, T)
    return pl.pallas_call(
        _uv_body,
        grid=(B, T // tm),
        in_specs=[pl.BlockSpec((None, H, tm, R), lambda b, r: (b, 0, r, 0)),
                  _const_spec(wuv_t.shape), _const_spec((1, MLA_WIDTH))],
        out_specs=_grouped_spec(tm, MLA_WIDTH),
        out_shape=jax.ShapeDtypeStruct((B, T, MLA_WIDTH), BF16),
        compiler_params=_params(dimension_semantics=("parallel", "parallel")),
        name="uv_norm",
    )(ctx, wuv_t, out_norm)


def _outproj_body(or_ref, gate_ref, om_ref, x_ref, g1_ref, w_ref, lg_ref, lb_ref, o_ref):
    o_r = (or_ref[...] * gate_ref[...]).astype(BF16)
    mixed = jnp.dot(o_r, w_ref[:RWKV_WIDTH, :], preferred_element_type=F32)
    mixed = mixed + jnp.dot(om_ref[...], w_ref[RWKV_WIDTH:, :], preferred_element_type=F32)
    y = DEEPNORM_ALPHA * x_ref[...] + g1_ref[...] * mixed
    o_ref[...] = _layernorm(y, lg_ref[...], lb_ref[...])


def _out_proj(o_r, gate, o_m, x, mod, w_out, ln_g, ln_b):
    B, T, D = x.shape
    tm = min(256, T)
    rw = _tmajor_spec(tm, RWKV_WIDTH, RWKV_WIDTH)
    return pl.pallas_call(
        _outproj_body,
        grid=(B, T // tm),
        in_specs=[rw, rw, _grouped_spec(tm, MLA_WIDTH), _grouped_spec(tm, D),
                  _mod_spec(mod, tm, 2), _const_spec((D, D)), _const_spec((1, D)), _const_spec((1, D))],
        out_specs=_grouped_spec(tm, D),
        out_shape=jax.ShapeDtypeStruct((B, T, D), F32),
        compiler_params=_params(dimension_semantics=("parallel", "parallel")),
        name="out_proj_ln1",
    )(o_r.reshape(T, B * RWKV_WIDTH), gate.reshape(T, B * RWKV_WIDTH), o_m, x, mod, w_out, ln_g, ln_b)


def _ffn_body(x_ref, sc_ref, sh_ref, g2_ref, wu_ref, wd_ref, lg_ref, lb_ref, o_ref, h_scr, acc_scr):
    f = pl.program_id(2)

    @pl.when(f == 0)
    def _():
        h_scr[...] = (x_ref[...] * (1.0 + sc_ref[...]) + sh_ref[...]).astype(BF16)
        acc_scr[...] = jnp.zeros(acc_scr.shape, F32)

    u = jnp.maximum(jnp.dot(h_scr[...], wu_ref[...], preferred_element_type=F32), 0.0)
    acc_scr[...] += jnp.dot((u * u).astype(BF16), wd_ref[...], preferred_element_type=F32)

    @pl.when(f == pl.num_programs(2) - 1)
    def _():
        y = DEEPNORM_ALPHA * x_ref[...] + g2_ref[...] * acc_scr[...]
        o_ref[...] = _layernorm(y, lg_ref[...], lb_ref[...])


def _ffn(x, mod, w_up, w_down, ln_g, ln_b):
    B, T, D = x.shape
    tm = min(512, T)
    tf = 1024
    return pl.pallas_call(
        _ffn_body,
        grid=(B, T // tm, D_FF // tf),
        in_specs=[_grouped_spec(tm, D), _mod_spec(mod, tm, 4), _mod_spec(mod, tm, 3), _mod_spec(mod, tm, 5),
                  pl.BlockSpec((D, tf), lambda b, r, f: (0, f)),
                  pl.BlockSpec((tf, D), lambda b, r, f: (f, 0)),
                  _const_spec((1, D)), _const_spec((1, D))],
        out_specs=_grouped_spec(tm, D),
        out_shape=jax.ShapeDtypeStruct((B, T, D), F32),
        scratch_shapes=[pltpu.VMEM((tm, D), BF16), pltpu.VMEM((tm, D), F32)],
        compiler_params=_params(dimension_semantics=("parallel", "parallel", "arbitrary")),
        name="ffn_ln2",
    )(x, mod, mod, mod, w_up, w_down, ln_g, ln_b)


def _rope_tables(pos):
    half = MLA_ROPE // 2
    inv = ROPE_THETA ** (-jnp.arange(half, dtype=F32) / half)
    ang = pos.astype(F32)[:, None] * inv
    cos, sin = jnp.cos(ang), jnp.sin(ang)
    cos_t = jnp.tile(jnp.concatenate([cos, cos], -1), (1, MLA_HEADS))
    sin_t = jnp.tile(jnp.concatenate([-sin, sin], -1), (1, MLA_HEADS))
    return cos_t[None], sin_t[None]


SEQ_PER_GROUP = LANES // RWKV_HEADS


class _PromptLanes:
    @staticmethod
    def to_lanes(x):
        T, B, _ = x.shape
        x = x.reshape(T, B // SEQ_PER_GROUP, SEQ_PER_GROUP, RWKV_HEADS, RWKV_HEAD)
        return x.transpose(0, 1, 4, 2, 3).reshape(T, B // SEQ_PER_GROUP, RWKV_HEAD, LANES)

    @staticmethod
    def from_lanes(o):
        T, G = o.shape[:2]
        o = o.reshape(T, G, RWKV_HEAD, SEQ_PER_GROUP, RWKV_HEADS)
        return o.transpose(0, 1, 3, 4, 2).reshape(T, G * SEQ_PER_GROUP, RWKV_WIDTH)

    @staticmethod
    def tile(p, n_seq):
        return jnp.tile(p.reshape(RWKV_HEADS, RWKV_HEAD).T, (1, SEQ_PER_GROUP))[None]

    @staticmethod
    def state_to_lanes(s):
        B = s.shape[0]
        s = s.reshape(B // SEQ_PER_GROUP, LANES, RWKV_HEAD, RWKV_HEAD)
        return s.transpose(0, 3, 2, 1)

    @staticmethod
    def state_from_lanes(s):
        G = s.shape[0]
        return s.transpose(0, 3, 2, 1).reshape(G * SEQ_PER_GROUP, RWKV_HEADS, RWKV_HEAD, RWKV_HEAD)


class _SampleLanes:
    @staticmethod
    def to_lanes(x):
        Td, S, _ = x.shape
        x = x.reshape(Td, S // LANES, LANES, RWKV_HEADS, RWKV_HEAD)
        return x.transpose(0, 1, 3, 4, 2).reshape(Td, (S // LANES) * RWKV_HEADS, RWKV_HEAD, LANES)

    @staticmethod
    def from_lanes(o):
        Td, G = o.shape[:2]
        o = o.reshape(Td, G // RWKV_HEADS, RWKV_HEADS, RWKV_HEAD, LANES)
        return o.transpose(0, 1, 4, 2, 3).reshape(Td, (G // RWKV_HEADS) * LANES, RWKV_WIDTH)

    @staticmethod
    def tile(p, n_seq):
        t = jnp.broadcast_to(p.reshape(RWKV_HEADS, RWKV_HEAD, 1), (RWKV_HEADS, RWKV_HEAD, LANES))
        return jnp.tile(t, (n_seq // LANES, 1, 1))

    @staticmethod
    def state_to_lanes(s):
        S = s.shape[0]
        s = s.reshape(S // LANES, LANES, RWKV_HEADS, RWKV_HEAD, RWKV_HEAD)
        return s.transpose(0, 2, 4, 3, 1).reshape((S // LANES) * RWKV_HEADS, RWKV_HEAD, RWKV_HEAD, LANES)

    @staticmethod
    def state_from_lanes(s):
        G = s.shape[0]
        s = s.reshape(G // RWKV_HEADS, RWKV_HEADS, RWKV_HEAD, RWKV_HEAD, LANES)
        return s.transpose(0, 4, 1, 3, 2).reshape((G // RWKV_HEADS) * LANES, RWKV_HEADS, RWKV_HEAD, RWKV_HEAD)


def _rwkv_mix(proj, prev_p, state, lanes, wp):
    n_seq = proj.shape[1]
    r, k, v, w, a, g = _rwkv_pre(proj, prev_p, wp['mu'], wp['wa0'], wp['w_lora'], wp['g2'])
    tiles = [lanes.tile(p, n_seq) for p in wp['head_params']]
    o, s_out = _wkv(*(lanes.to_lanes(t) for t in (r, k, v, w, a)), tiles, lanes.state_to_lanes(state))
    return lanes.from_lanes(o), g, lanes.state_from_lanes(s_out)


def _layer_back(x, mod, o_r, gate, ctx, wp):
    o_m = _uv_norm(ctx, wp['wuv'], wp['out_norm'])
    x1 = _out_proj(o_r, gate, o_m, x, mod, wp['w_out'], wp['ln1_g'], wp['ln1_b'])
    return _ffn(x1, mod, wp['w_up'], wp['w_down'], wp['ln2_g'], wp['ln2_b'])


def kernel(x_prompt, x_sample, c_prompt, c_sample, cache_latent, cache_krope, state_wkv, state_shift, page_table, w_ada, b_ada, w_in, rwkv_mu, rwkv_w0, rwkv_w2, rwkv_a0, rwkv_a2, rwkv_g2, rwkv_k_k, rwkv_k_a, rwkv_r_k, rwkv_lnx_g, rwkv_lnx_b, mla_q_norm, mla_w_q_up, mla_kv_norm, mla_w_uk, mla_w_uv, mla_out_norm, w_out, ln1_g, ln1_b, w_up, w_down, ln2_g, ln2_b):
    B, T, D = x_prompt.shape
    S, Td, _ = x_sample.shape
    past = page_table.shape[1] * cache_latent.shape[2]
    l = 0

    w_lora = jnp.zeros((256, 2 * RWKV_WIDTH), F32)
    w_lora = w_lora.at[:DECAY_LORA, :RWKV_WIDTH].set(rwkv_w2[l])
    w_lora = w_lora.at[DECAY_LORA:DECAY_LORA + AAA_LORA, RWKV_WIDTH:].set(rwkv_a2[l])
    wq = mla_w_q_up[l].reshape(Q_RANK, MLA_HEADS, MLA_NOPE + MLA_ROPE)
    wq_p = jnp.concatenate([wq[:, :, :MLA_NOPE].reshape(Q_RANK, -1), wq[:, :, MLA_NOPE:].reshape(Q_RANK, -1)], -1)
    row = lambda p: p.reshape(1, -1)
    wp = dict(
        w_in=_perm_cols(w_in[l]).astype(BF16),
        mu=row(_perm_cols(rwkv_mu[l])),
        wa0=row(jnp.concatenate([rwkv_w0[l], rwkv_a0[l]])),
        w_lora=w_lora.astype(BF16),
        g2=rwkv_g2[l].astype(BF16),
        head_params=[rwkv_k_k[l], rwkv_k_a[l], rwkv_r_k[l].reshape(-1), rwkv_lnx_g[l], rwkv_lnx_b[l]],
        q_norm=row(mla_q_norm[l]), wq=wq_p.astype(BF16),
        wuk=mla_w_uk[l].transpose(1, 2, 0).astype(BF16),
        kv_norm=row(mla_kv_norm[l]),
        wuv=mla_w_uv[l].transpose(1, 0, 2).astype(BF16),
        out_norm=row(mla_out_norm[l]),
        w_out=w_out[l].astype(BF16), ln1_g=row(ln1_g[l]), ln1_b=row(ln1_b[l]),
        w_up=w_up[l].astype(BF16), w_down=w_down[l].astype(BF16), ln2_g=row(ln2_g[l]), ln2_b=row(ln2_b[l]),
    )

    mod = _ada(jnp.concatenate([c_prompt, c_sample], 0), w_ada[l], b_ada[l])
    mod_p = mod[:B, None, :]
    mod_s = jnp.tile(mod[B:], (Td, 1))[None]

    s0_p = jnp.zeros((B, RWKV_HEADS, RWKV_HEAD, RWKV_HEAD), F32)
    prev0 = jnp.zeros((B, RW_COLS), F32)
    proj_p = _in_proj(x_prompt, mod_p, wp['w_in'])
    o_r, gate_p, wkv_p = _rwkv_mix(proj_p, prev0, s0_p, _PromptLanes, wp)
    cos_t, sin_t = _rope_tables(jnp.arange(T))
    q4, kc, lat_p, kr_p = _mla_proj(proj_p, cos_t, sin_t, wp['q_norm'], wp['wq'], wp['wuk'], wp['kv_norm'])
    last_p = _unperm_shift_cols(proj_p[T - 1, :, :RW_COLS])
    ctx_p = _prompt_attn(q4, kc)
    y_p = _layer_back(x_prompt, mod_p, o_r, gate_p, ctx_p, wp)

    xs = x_sample.transpose(1, 0, 2).reshape(1, Td * S, D)
    pos_s = jnp.repeat(past + jnp.arange(Td), S)
    prev_s = _perm_cols(state_shift[l])
    proj = _in_proj(xs, mod_s, wp['w_in']).reshape(Td, S, NP_COLS)
    o_rs, gate_s, wkv_s = _rwkv_mix(proj, prev_s, state_wkv[l], _SampleLanes, wp)
    o_rs = o_rs.reshape(Td * S, 1, RWKV_WIDTH)
    gate_s = gate_s.reshape(Td * S, 1, RWKV_WIDTH)
    cos_t, sin_t = _rope_tables(pos_s)
    q4s, _, lat_s, kr_s = _mla_proj(proj.reshape(Td * S, 1, NP_COLS), cos_t, sin_t, wp['q_norm'], wp['wq'],
                                    wp['wuk'], wp['kv_norm'])
    last_s = _unperm_shift_cols(proj[Td - 1, :, :RW_COLS])
    qk = KV_RANK + MLA_ROPE
    q_s = q4s.reshape(MLA_HEADS, Td, S, qk).transpose(2, 0, 1, 3).reshape(S, MLA_HEADS * Td, qk)
    lat_s = lat_s.reshape(Td, S, KV_RANK).transpose(1, 0, 2)
    kr_s = kr_s.reshape(Td, S, MLA_ROPE).transpose(1, 0, 2)
    pad16 = lambda t: jnp.pad(t, ((0, 0), (0, 16 - Td), (0, 0)))
    ctx_s = _sample_attn(q_s, pad16(lat_s), pad16(kr_s), cache_latent[l], cache_krope[l].transpose(0, 2, 1),
                         page_table)
    ctx_s = ctx_s.reshape(S, MLA_HEADS, Td, KV_RANK).transpose(1, 2, 0, 3).reshape(1, MLA_HEADS, Td * S, KV_RANK)
    y_s = _layer_back(xs, mod_s, o_rs, gate_s, ctx_s, wp)
    y_s = y_s.reshape(Td, S, D).transpose(1, 0, 2)

    return (y_p, y_s, lat_p[None], kr_p[None], wkv_p[None], last_p[None],
            lat_s[None], kr_s[None], wkv_s[None], last_s[None])
```

```python
import functools

import jax
import jax.numpy as jnp
from jax import lax
from jax.experimental import pallas as pl
from jax.experimental.pallas import tpu as pltpu

F32 = jnp.float32
BF16 = jnp.bfloat16

D_MODEL = 2048
RWKV_WIDTH = 1024
RWKV_HEAD = 64
RWKV_HEADS = 16
DECAY_LORA = 96
AAA_LORA = 96
GATE_LORA = 256
MLA_V = 128
MLA_HEADS = 8
MLA_WIDTH = 1024
MLA_NOPE = 128
MLA_ROPE = 64
Q_RANK = 512
KV_RANK = 256
D_FF = 4 * D_MODEL
ROPE_THETA = 10000.0
N_SHIFT = 3 * RWKV_WIDTH + DECAY_LORA + AAA_LORA + GATE_LORA
N_IN = N_SHIFT + Q_RANK + KV_RANK + MLA_ROPE
DEPTH = 1
DEEPNORM_ALPHA = (2.0 * DEPTH) ** 0.25
LN_EPS = 1e-5
RMS_EPS = 1e-6
GN_EPS = 64e-5
ATTN_SCALE = (MLA_NOPE + MLA_ROPE) ** -0.5
NEG_BIG = -1e30

LANES = 128
VMEM_LIMIT = 56 * 1024 * 1024

OFF_R, OFF_K, OFF_V = 0, RWKV_WIDTH, 2 * RWKV_WIDTH
OFF_GD = 3 * RWKV_WIDTH
OFF_WA = OFF_GD + GATE_LORA
RW_COLS = OFF_WA + 256
OFF_CQ = RW_COLS
OFF_CKV = OFF_CQ + Q_RANK
OFF_KR = OFF_CKV + KV_RANK
NP_COLS = 4608
MLA_BLK = 512


def _perm_cols(a):
    z = lambda n: jnp.zeros(a.shape[:-1] + (n,), a.dtype)
    o_wd = 3 * RWKV_WIDTH
    o_gd = o_wd + DECAY_LORA + AAA_LORA
    pieces = [a[..., :o_wd], a[..., o_gd:N_SHIFT], a[..., o_wd:o_gd], z(256 - DECAY_LORA - AAA_LORA)]
    if a.shape[-1] == N_IN:
        pieces += [a[..., N_SHIFT:N_IN], z(NP_COLS - OFF_KR - MLA_ROPE)]
    return jnp.concatenate(pieces, -1)


def _unperm_shift_cols(a):
    return jnp.concatenate([a[..., :OFF_GD], a[..., OFF_WA:OFF_WA + DECAY_LORA + AAA_LORA],
                            a[..., OFF_GD:OFF_WA]], -1)


def _params(**kw):
    return pltpu.CompilerParams(vmem_limit_bytes=VMEM_LIMIT, **kw)


def _sigmoid(x):
    return 1.0 / (1.0 + jnp.exp(-x))


def _layernorm(x, g, b):
    mu = jnp.mean(x, -1, keepdims=True)
    xc = x - mu
    var = jnp.mean(xc * xc, -1, keepdims=True)
    return xc * lax.rsqrt(var + LN_EPS) * g + b


def _rmsnorm(x, g):
    return x * lax.rsqrt(jnp.mean(x * x, -1, keepdims=True) + RMS_EPS) * g


def _ada_body(c_ref, w_ref, b_ref, o_ref):
    c = c_ref[...]
    a = (c * _sigmoid(c)).astype(BF16)
    o_ref[...] = jnp.dot(a, w_ref[...].astype(BF16), preferred_element_type=F32) + b_ref[...]


def _ada(c, w_ada, b_ada):
    rows, d = c.shape
    n = w_ada.shape[1]
    tn = 1024
    return pl.pallas_call(
        _ada_body,
        grid=(n // tn,),
        in_specs=[pl.BlockSpec((rows, d), lambda j: (0, 0)),
                  pl.BlockSpec((d, tn), lambda j: (0, j)),
                  pl.BlockSpec((1, tn), lambda j: (0, j))],
        out_specs=pl.BlockSpec((rows, tn), lambda j: (0, j)),
        out_shape=jax.ShapeDtypeStruct((rows, n), F32),
        compiler_params=_params(dimension_semantics=("parallel",)),
        name="ada_mod",
    )(c, w_ada, b_ada.reshape(1, n))


def _grouped_spec(tm, d, col=0):
    return pl.BlockSpec((None, tm, d), lambda b, r, *_: (b, r, col))


def _tmajor_spec(tm, d, width, col=0):
    per_b = width // d
    return pl.BlockSpec((tm, d), lambda b, r, *_: (r, b * per_b + col))


def _mod_spec(mod, tm, chunk):
    if mod.shape[1] == 1:
        return pl.BlockSpec((None, 1, D_MODEL), lambda b, r, *_: (b, 0, chunk))
    return pl.BlockSpec((None, tm, D_MODEL), lambda b, r, *_: (b, r, chunk))


def _const_spec(shape):
    nd = len(shape)
    return pl.BlockSpec(shape, lambda *_: (0,) * nd)


def _inproj_body(x_ref, sc_ref, sh_ref, w_ref, o_ref, h_scr):
    @pl.when(pl.program_id(2) == 0)
    def _():
        h = x_ref[...] * (1.0 + sc_ref[...]) + sh_ref[...]
        h_scr[...] = h.astype(BF16)

    o_ref[...] = jnp.dot(h_scr[...], w_ref[...], preferred_element_type=F32)


def _in_proj(x, mod, w_in_p):
    B, T, D = x.shape
    tm = min(512, T)
    nj = 2
    tn = NP_COLS // nj
    return pl.pallas_call(
        _inproj_body,
        grid=(B, T // tm, nj),
        in_specs=[_grouped_spec(tm, D), _mod_spec(mod, tm, 1), _mod_spec(mod, tm, 0),
                  pl.BlockSpec((D, tn), lambda b, r, j: (0, j))],
        out_specs=pl.BlockSpec((tm, tn), lambda b, r, j: (r, b * nj + j)),
        out_shape=jax.ShapeDtypeStruct((T, B * NP_COLS), F32),
        scratch_shapes=[pltpu.VMEM((tm, D), BF16)],
        compiler_params=_params(dimension_semantics=("parallel", "parallel", "arbitrary")),
        name="in_proj",
    )(x, mod, mod, w_in_p).reshape(T, B, NP_COLS)


def _rwkv_pre_body(p_ref, pb_ref, prev_ref, mu_ref, wa0_ref, wl_ref, g2_ref,
                   r_ref, k_ref, v_ref, w_ref, a_ref, g_ref):
    tt, nb, _ = p_ref.shape
    p = p_ref[...]
    first = jnp.where(pl.program_id(0) == 0, prev_ref[...], pb_ref[0])
    shifted = jnp.concatenate([first[None], p[:-1]], axis=0)
    z = (p + mu_ref[...] * (shifted - p)).reshape(tt * nb, RW_COLS)
    out3 = lambda t: t.reshape(tt, nb, RWKV_WIDTH)
    r_ref[...] = out3(z[:, OFF_R:OFF_R + RWKV_WIDTH])
    k_ref[...] = out3(z[:, OFF_K:OFF_K + RWKV_WIDTH])
    v_ref[...] = out3(z[:, OFF_V:OFF_V + RWKV_WIDTH])
    wa = z[:, OFF_WA:OFF_WA + 256]
    lane = lax.broadcasted_iota(jnp.int32, wa.shape, 1)
    lora_in = jnp.where(lane < DECAY_LORA, jnp.tanh(wa), wa).astype(BF16)
    pre = jnp.dot(lora_in, wl_ref[...], preferred_element_type=F32) + wa0_ref[...]
    u = pre[:, :RWKV_WIDTH]
    w_log = jnp.minimum(u, 0.0) - jnp.log1p(jnp.exp(-jnp.abs(u))) - 0.5
    w_ref[...] = out3(jnp.exp(-jnp.exp(w_log)))
    a_ref[...] = out3(_sigmoid(pre[:, RWKV_WIDTH:]))
    gd = _sigmoid(z[:, OFF_GD:OFF_GD + GATE_LORA]).astype(BF16)
    g_ref[...] = out3(jnp.dot(gd, g2_ref[...], preferred_element_type=F32))


def _rwkv_pre(proj, prev_p, mu_p, wa0, w_lora, g2):
    T, B, _ = proj.shape
    tt = max(min(256 // B, T), 1)
    blk = pl.BlockSpec((tt, B, RWKV_WIDTH), lambda i: (i, 0, 0))
    out = jax.ShapeDtypeStruct((T, B, RWKV_WIDTH), F32)
    return pl.pallas_call(
        _rwkv_pre_body,
        grid=(T // tt,),
        in_specs=[pl.BlockSpec((tt, B, RW_COLS), lambda i: (i, 0, 0)),
                  pl.BlockSpec((1, B, RW_COLS), lambda i: (jnp.maximum(i * tt - 1, 0), 0, 0)),
                  _const_spec((B, RW_COLS)), _const_spec((1, RW_COLS)), _const_spec((1, 2 * RWKV_WIDTH)),
                  _const_spec((256, 2 * RWKV_WIDTH)), _const_spec((GATE_LORA, RWKV_WIDTH))],
        out_specs=[blk] * 6,
        out_shape=[out] * 6,
        compiler_params=_params(dimension_semantics=("parallel",)),
        name="rwkv_pre",
    )(proj, proj, prev_p, mu_p, wa0, w_lora, g2)


WKV_UNROLL = 8


def _wkv_body(r_ref, k_ref, v_ref, w_ref, a_ref, kk_p, ka_p, rk_p, lg_p, lb_p, s0_ref,
              o_ref, sout_ref, S, b_kk, b_wr, b_w, b_kka, b_k2):
    tt = r_ref.shape[0]
    N = RWKV_HEAD

    @pl.when(pl.program_id(1) == 0)
    def _():
        S[...] = s0_ref[...]

    def tok(t, carry):
        rT = r_ref[t]
        kT = k_ref[t]
        aT = a_ref[t]
        wT = w_ref[t]
        kk = kT * kk_p[...]
        nrm = jnp.sqrt(jnp.sum(kk * kk, axis=0, keepdims=True))
        kk = kk / jnp.maximum(nrm, 1e-12)
        k2 = kT * (1.0 + (aT - 1.0) * ka_p[...])
        kka = kk * aT
        b_kk[...] = kk
        b_wr[...] = wT * rT
        b_w[...] = wT
        b_kka[...] = kka
        b_k2[...] = k2
        c1 = jnp.sum(kka * rT, axis=0, keepdims=True)
        c2 = jnp.sum(k2 * rT, axis=0, keepdims=True)
        bonus = jnp.sum(rT * k2 * rk_p[...], axis=0, keepdims=True)

        def reduce_pass(j0, acc):
            sa, o1 = acc
            for jj in range(WKV_UNROLL):
                j = j0 * WKV_UNROLL + jj
                Sj = S[j]
                sa = sa + Sj * b_kk[pl.ds(j, 1), :]
                o1 = o1 + Sj * b_wr[pl.ds(j, 1), :]
            return sa, o1

        zero = jnp.zeros((N, LANES), F32)
        sa, o1 = lax.fori_loop(0, N // WKV_UNROLL, reduce_pass, (zero, zero))
        vT = v_ref[t]
        o = o1 - sa * c1 + vT * c2

        def update_pass(j0, c):
            for jj in range(WKV_UNROLL):
                j = j0 * WKV_UNROLL + jj
                S[j] = S[j] * b_w[pl.ds(j, 1), :] - sa * b_kka[pl.ds(j, 1), :] + vT * b_k2[pl.ds(j, 1), :]
            return c

        lax.fori_loop(0, N // WKV_UNROLL, update_pass, 0)

        mu = jnp.mean(o, axis=0, keepdims=True)
        oc = o - mu
        var = jnp.mean(oc * oc, axis=0, keepdims=True)
        on = oc * lax.rsqrt(var + GN_EPS) * lg_p[...] + lb_p[...]
        o_ref[t] = on + bonus * vT
        return carry

    lax.fori_loop(0, tt, tok, 0)

    @pl.when(pl.program_id(1) == pl.num_programs(1) - 1)
    def _():
        sout_ref[...] = S[...]


def _wkv(r, k, v, w, a, tiles, s0):
    T, G, N, _ = r.shape
    tt = min(32, T)
    tok_spec = pl.BlockSpec((tt, None, N, LANES), lambda g_, i: (i, g_, 0, 0))
    st_spec = pl.BlockSpec((None, N, N, LANES), lambda g_, i: (g_, 0, 0, 0))
    per_group = tiles[0].shape[0] > 1
    tile_spec = pl.BlockSpec((None, N, LANES), lambda g_, i: (g_ if per_group else 0, 0, 0))
    return pl.pallas_call(
        _wkv_body,
        grid=(G, T // tt),
        in_specs=[tok_spec] * 5 + [tile_spec] * 5 + [st_spec],
        out_specs=[tok_spec, st_spec],
        out_shape=[jax.ShapeDtypeStruct((T, G, N, LANES), F32), jax.ShapeDtypeStruct((G, N, N, LANES), F32)],
        scratch_shapes=[pltpu.VMEM((N, N, LANES), F32)] + [pltpu.VMEM((N, LANES), F32)] * 5,
        compiler_params=_params(dimension_semantics=("parallel", "arbitrary")),
        name="wkv_scan",
    )(r, k, v, w, a, *tiles, s0)


def _rot_half(x):
    n = x.shape[-1]
    lane = lax.broadcasted_iota(jnp.int32, x.shape, x.ndim - 1)
    fwd = pltpu.roll(x, n - MLA_ROPE // 2, axis=x.ndim - 1)
    bwd = pltpu.roll(x, MLA_ROPE // 2, axis=x.ndim - 1)
    return jnp.where(lane % MLA_ROPE < MLA_ROPE // 2, fwd, bwd)


def _mla_body(cq_ref, ckr_ref, cos_ref, sin_ref, qn_ref, wq_ref, wuk_ref, kvn_ref,
              q_ref, kc_ref, lat_ref, kr_ref, latt_ref):
    cq = cq_ref[...]
    q = jnp.dot(_rmsnorm(cq, qn_ref[...]).astype(BF16), wq_ref[...], preferred_element_type=F32)
    nope_w = MLA_HEADS * MLA_NOPE
    qr = q[:, nope_w:]
    q_rope = qr * cos_ref[...] + _rot_half(qr) * sin_ref[...]
    for h in range(MLA_HEADS):
        qn = q[:, h * MLA_NOPE:(h + 1) * MLA_NOPE].astype(BF16)
        q_ref[h, :, :KV_RANK] = jnp.dot(qn, wuk_ref[h], preferred_element_type=F32).astype(BF16)
        q_ref[h, :, KV_RANK:] = q_rope[:, h * MLA_ROPE:(h + 1) * MLA_ROPE].astype(BF16)
    ckr = ckr_ref[...]
    lat = _rmsnorm(ckr[:, :KV_RANK], kvn_ref[...])
    slab = ckr[:, KV_RANK:KV_RANK + LANES]
    kr = (slab * cos_ref[:, :LANES] + _rot_half(slab) * sin_ref[:, :LANES])[:, :MLA_ROPE]
    lat_ref[...] = lat
    kr_ref[...] = kr
    kc_ref[:, :KV_RANK] = lat.astype(BF16)
    kc_ref[:, KV_RANK:] = kr.astype(BF16)
    latt_ref[...] = lat.T.astype(BF16)


def _mla_proj(proj, cos_t, sin_t, q_norm, wq_p, wuk_t, kv_norm):
    T, B, _ = proj.shape
    tm = min(512, T)
    qk = KV_RANK + MLA_ROPE
    proj2 = proj.reshape(T, B * NP_COLS)
    tab = pl.BlockSpec((None, tm, MLA_HEADS * MLA_ROPE), lambda b, r: (0, r, 0))
    return pl.pallas_call(
        _mla_body,
        grid=(B, T // tm),
        in_specs=[_tmajor_spec(tm, MLA_BLK, NP_COLS, OFF_CQ // MLA_BLK),
                  _tmajor_spec(tm, MLA_BLK, NP_COLS, OFF_CKV // MLA_BLK),
                  tab, tab, _const_spec((1, Q_RANK)), _const_spec(wq_p.shape), _const_spec(wuk_t.shape),
                  _const_spec((1, KV_RANK))],
        out_specs=[pl.BlockSpec((None, MLA_HEADS, tm, qk), lambda b, r: (b, 0, r, 0)),
                   _grouped_spec(tm, qk), _grouped_spec(tm, KV_RANK), _grouped_spec(tm, MLA_ROPE),
                   pl.BlockSpec((None, KV_RANK, tm), lambda b, r: (b, 0, r))],
        out_shape=[jax.ShapeDtypeStruct((B, MLA_HEADS, T, qk), BF16),
                   jax.ShapeDtypeStruct((B, T, qk), BF16),
                   jax.ShapeDtypeStruct((B, T, KV_RANK), F32),
                   jax.ShapeDtypeStruct((B, T, MLA_ROPE), F32),
                   jax.ShapeDtypeStruct((B, KV_RANK, T), BF16)],
        compiler_params=_params(dimension_semantics=("parallel", "parallel")),
        name="mla_proj",
    )(proj2, proj2, cos_t, sin_t, q_norm, wq_p, wuk_t, kv_norm)


def _pattn_body(q_ref, k_ref, vt_ref, o_ref, m_scr, l_scr, acc_scr, *, tq, tk):
    i = pl.program_id(1)
    j = pl.program_id(2)
    last_j = ((i + 1) * tq - 1) // tk

    @pl.when(j == 0)
    def _():
        m_scr[...] = jnp.full(m_scr.shape, NEG_BIG, F32)
        l_scr[...] = jnp.zeros(l_scr.shape, F32)
        acc_scr[...] = jnp.zeros(acc_scr.shape, F32)

    def step(masked):
        k = k_ref[...]
        vt = vt_ref[...]
        if masked:
            kpos = j * tk + lax.broadcasted_iota(jnp.int32, (tk, tq), 0)
            qpos = i * tq + lax.broadcasted_iota(jnp.int32, (tk, tq), 1)
            keep = kpos <= qpos
        for h in range(MLA_HEADS):
            st = lax.dot_general(k, q_ref[h], (((1,), (1,)), ((), ())),
                                 preferred_element_type=F32) * ATTN_SCALE
            if masked:
                st = jnp.where(keep, st, NEG_BIG)
            m_prev = m_scr[h]
            m_new = jnp.maximum(m_prev, jnp.max(st, axis=0, keepdims=True))
            alpha = jnp.exp(m_prev - m_new)
            pt = jnp.exp(st - m_new)
            l_scr[h] = alpha * l_scr[h] + jnp.sum(pt, axis=0, keepdims=True)
            acc_scr[h] = alpha * acc_scr[h] + jnp.dot(vt, pt.astype(BF16), preferred_element_type=F32)
            m_scr[h] = m_new

    crosses_diagonal = (j + 1) * tk - 1 > i * tq
    pl.when(jnp.logical_and(j <= last_j, crosses_diagonal))(functools.partial(step, True))
    pl.when(jnp.logical_and(j <= last_j, jnp.logical_not(crosses_diagonal)))(functools.partial(step, False))

    @pl.when(j == pl.num_programs(2) - 1)
    def _():
        for h in range(MLA_HEADS):
            o_ref[h] = (acc_scr[h] / l_scr[h]).T.astype(BF16)


def _prompt_attn(q4, kc, lat_t):
    B, H, T, qk = q4.shape
    tq = min(256, T)
    tk = min(512, T)
    body = functools.partial(_pattn_body, tq=tq, tk=tk)
    kv_block = lambda i, j: jnp.minimum(j, ((i + 1) * tq - 1) // tk)
    return pl.pallas_call(
        body,
        grid=(B, T // tq, T // tk),
        in_specs=[pl.BlockSpec((None, H, tq, qk), lambda b, i, j: (b, 0, i, 0)),
                  pl.BlockSpec((None, tk, qk), lambda b, i, j: (b, kv_block(i, j), 0)),
                  pl.BlockSpec((None, KV_RANK, tk), lambda b, i, j: (b, 0, kv_block(i, j)))],
        out_specs=pl.BlockSpec((None, H, tq, KV_RANK), lambda b, i, j: (b, 0, i, 0)),
        out_shape=jax.ShapeDtypeStruct((B, H, T, KV_RANK), BF16),
        scratch_shapes=[pltpu.VMEM((H, 1, tq), F32), pltpu.VMEM((H, 1, tq), F32),
                        pltpu.VMEM((H, KV_RANK, tq), F32)],
        compiler_params=_params(dimension_semantics=("parallel", "parallel", "arbitrary")),
        name="prompt_attn",
    )(q4, kc, lat_t)


SAMPLE_PAGES_PER_STEP = 64


def _sattn_body(pt_ref, q_ref, latn_ref, krn_ref, *rest, n_pg, n_new):
    lat_refs = rest[:n_pg]
    kr_refs = rest[n_pg:2 * n_pg]
    o_ref, m_scr, l_scr, acc_scr = rest[2 * n_pg:]
    c = pl.program_id(1)
    nt = (((1,), (1,)), ((), ()))

    @pl.when(c == 0)
    def _():
        m_scr[...] = jnp.full(m_scr.shape, NEG_BIG, F32)
        l_scr[...] = jnp.zeros(l_scr.shape, F32)
        acc_scr[...] = jnp.zeros(acc_scr.shape, F32)

    q = q_ref[...]
    q_lat = q[:, :KV_RANK]
    q_rope = q[:, KV_RANK:]

    def update(s, vals):
        m_prev = m_scr[...]
        m_new = jnp.maximum(m_prev, jnp.max(s, axis=1, keepdims=True))
        alpha = jnp.exp(m_prev - m_new)
        p = jnp.exp(s - m_new)
        l_scr[...] = alpha * l_scr[...] + jnp.sum(p, axis=1, keepdims=True)
        acc_scr[...] = alpha * acc_scr[...] + jnp.dot(p.astype(BF16), vals, preferred_element_type=F32)
        m_scr[...] = m_new

    lat_all = jnp.concatenate([lat_refs[p][...].astype(BF16) for p in range(n_pg)], axis=0)
    krt_all = jnp.concatenate([kr_refs[p][...].astype(BF16) for p in range(n_pg)], axis=1)
    s = lax.dot_general(q_lat, lat_all, nt, preferred_element_type=F32)
    s = (s + jnp.dot(q_rope, krt_all, preferred_element_type=F32)) * ATTN_SCALE
    update(s, lat_all)

    @pl.when(c == pl.num_programs(1) - 1)
    def _():
        latn = latn_ref[...].astype(BF16)
        sn = lax.dot_general(q_lat, latn, nt, preferred_element_type=F32)
        sn = (sn + lax.dot_general(q_rope, krn_ref[...].astype(BF16), nt, preferred_element_type=F32)) * ATTN_SCALE
        row_t = lax.broadcasted_iota(jnp.int32, sn.shape, 0) % n_new
        col_t = lax.broadcasted_iota(jnp.int32, sn.shape, 1)
        sn = jnp.where(col_t <= row_t, sn, NEG_BIG)
        update(sn, latn)
        o_ref[...] = acc_scr[...] / l_scr[...]


def _sample_attn(q, latn, krn, cache_lat, cache_kr, page_table):
    S, rows, qk = q.shape
    n_pages = page_table.shape[1]
    page = cache_lat.shape[1]
    n_pg = min(SAMPLE_PAGES_PER_STEP, n_pages)
    n_new = rows // MLA_HEADS
    pad_new = latn.shape[1]
    lat_specs = [pl.BlockSpec((None, page, KV_RANK),
                              lambda s, c, pt, p=p: (pt[s, c * n_pg + p], 0, 0)) for p in range(n_pg)]
    kr_specs = [pl.BlockSpec((None, MLA_ROPE, page),
                             lambda s, c, pt, p=p: (pt[s, c * n_pg + p], 0, 0)) for p in range(n_pg)]
    grid_spec = pltpu.PrefetchScalarGridSpec(
        num_scalar_prefetch=1,
        grid=(S, n_pages // n_pg),
        in_specs=[pl.BlockSpec((None, rows, qk), lambda s, c, pt: (s, 0, 0)),
                  pl.BlockSpec((None, pad_new, KV_RANK), lambda s, c, pt: (s, 0, 0)),
                  pl.BlockSpec((None, pad_new, MLA_ROPE), lambda s, c, pt: (s, 0, 0))] + lat_specs + kr_specs,
        out_specs=pl.BlockSpec((None, rows, KV_RANK), lambda s, c, pt: (s, 0, 0)),
        scratch_shapes=[pltpu.VMEM((rows, 1), F32), pltpu.VMEM((rows, 1), F32),
                        pltpu.VMEM((rows, KV_RANK), F32)],
    )
    return pl.pallas_call(
        functools.partial(_sattn_body, n_pg=n_pg, n_new=n_new),
        grid_spec=grid_spec,
        out_shape=jax.ShapeDtypeStruct((S, rows, KV_RANK), F32),
        compiler_params=_params(dimension_semantics=("parallel", "arbitrary")),
        name="sample_attn",
    )(page_table, q, latn, krn, *([cache_lat] * n_pg), *([cache_kr] * n_pg))


def _uv_body(ctx_ref, wuv_ref, on_ref, o_ref):
    parts = [jnp.dot(ctx_ref[h].astype(BF16), wuv_ref[h], preferred_element_type=F32)
             for h in range(MLA_HEADS)]
    om = jnp.concatenate(parts, axis=1)
    o_ref[...] = _rmsnorm(om, on_ref[...]).astype(BF16)


def _uv_norm(ctx, wuv_t, out_norm):
    B, H, T, R = ctx.shape
    tm = min(512, T)
    return pl.pallas_call(
        _uv_body,
        grid=(B, T // tm),
        in_specs=[pl.BlockSpec((None, H, tm, R), lambda b, r: (b, 0, r, 0)),
                  _const_spec(wuv_t.shape), _const_spec((1, MLA_WIDTH))],
        out_specs=_grouped_spec(tm, MLA_WIDTH),
        out_shape=jax.ShapeDtypeStruct((B, T, MLA_WIDTH), BF16),
        compiler_params=_params(dimension_semantics=("parallel", "parallel")),
        name="uv_norm",
    )(ctx, wuv_t, out_norm)


def _outproj_body(or_ref, gate_ref, om_ref, x_ref, g1_ref, w_ref, lg_ref, lb_ref, o_ref):
    o_r = (or_ref[...] * gate_ref[...]).astype(BF16)
    mixed = jnp.dot(o_r, w_ref[:RWKV_WIDTH, :], preferred_element_type=F32)
    mixed = mixed + jnp.dot(om_ref[...], w_ref[RWKV_WIDTH:, :], preferred_element_type=F32)
    y = DEEPNORM_ALPHA * x_ref[...] + g1_ref[...] * mixed
    o_ref[...] = _layernorm(y, lg_ref[...], lb_ref[...])


def _out_proj(o_r, gate, o_m, x, mod, w_out, ln_g, ln_b):
    B, T, D = x.shape
    tm = min(256, T)
    rw = _tmajor_spec(tm, RWKV_WIDTH, RWKV_WIDTH)
    return pl.pallas_call(
        _outproj_body,
        grid=(B, T // tm),
        in_specs=[rw, rw, _grouped_spec(tm, MLA_WIDTH), _grouped_spec(tm, D),
                  _mod_spec(mod, tm, 2), _const_spec((D, D)), _const_spec((1, D)), _const_spec((1, D))],
        out_specs=_grouped_spec(tm, D),
        out_shape=jax.ShapeDtypeStruct((B, T, D), F32),
        compiler_params=_params(dimension_semantics=("parallel", "parallel")),
        name="out_proj_ln1",
    )(o_r.reshape(T, B * RWKV_WIDTH), gate.reshape(T, B * RWKV_WIDTH), o_m, x, mod, w_out, ln_g, ln_b)


def _ffn_body(x_ref, sc_ref, sh_ref, g2_ref, wu_ref, wd_ref, lg_ref, lb_ref, o_ref, h_scr, acc_scr):
    f = pl.program_id(2)

    @pl.when(f == 0)
    def _():
        h_scr[...] = (x_ref[...] * (1.0 + sc_ref[...]) + sh_ref[...]).astype(BF16)
        acc_scr[...] = jnp.zeros(acc_scr.shape, F32)

    u = jnp.maximum(jnp.dot(h_scr[...], wu_ref[...], preferred_element_type=F32), 0.0)
    acc_scr[...] += jnp.dot((u * u).astype(BF16), wd_ref[...], preferred_element_type=F32)

    @pl.when(f == pl.num_programs(2) - 1)
    def _():
        y = DEEPNORM_ALPHA * x_ref[...] + g2_ref[...] * acc_scr[...]
        o_ref[...] = _layernorm(y, lg_ref[...], lb_ref[...])


def _ffn(x, mod, w_up, w_down, ln_g, ln_b):
    B, T, D = x.shape
    tm = min(512, T)
    tf = 1024
    return pl.pallas_call(
        _ffn_body,
        grid=(B, T // tm, D_FF // tf),
        in_specs=[_grouped_spec(tm, D), _mod_spec(mod, tm, 4), _mod_spec(mod, tm, 3), _mod_spec(mod, tm, 5),
                  pl.BlockSpec((D, tf), lambda b, r, f: (0, f)),
                  pl.BlockSpec((tf, D), lambda b, r, f: (f, 0)),
                  _const_spec((1, D)), _const_spec((1, D))],
        out_specs=_grouped_spec(tm, D),
        out_shape=jax.ShapeDtypeStruct((B, T, D), F32),
        scratch_shapes=[pltpu.VMEM((tm, D), BF16), pltpu.VMEM((tm, D), F32)],
        compiler_params=_params(dimension_semantics=("parallel", "parallel", "arbitrary")),
        name="ffn_ln2",
    )(x, mod, mod, mod, w_up, w_down, ln_g, ln_b)


def _rope_tables(pos):
    half = MLA_ROPE // 2
    inv = ROPE_THETA ** (-jnp.arange(half, dtype=F32) / half)
    ang = pos.astype(F32)[:, None] * inv
    cos, sin = jnp.cos(ang), jnp.sin(ang)
    cos_t = jnp.tile(jnp.concatenate([cos, cos], -1), (1, MLA_HEADS))
    sin_t = jnp.tile(jnp.concatenate([-sin, sin], -1), (1, MLA_HEADS))
    return cos_t[None], sin_t[None]


SEQ_PER_GROUP = LANES // RWKV_HEADS


class _PromptLanes:
    @staticmethod
    def to_lanes(x):
        T, B, _ = x.shape
        return x.reshape(T, B // SEQ_PER_GROUP, LANES, RWKV_HEAD).transpose(0, 1, 3, 2)

    @staticmethod
    def from_lanes(o):
        T, G = o.shape[:2]
        return o.transpose(0, 1, 3, 2).reshape(T, G * SEQ_PER_GROUP, RWKV_WIDTH)

    @staticmethod
    def tile(p, n_seq):
        return jnp.tile(p.reshape(RWKV_HEADS, RWKV_HEAD).T, (1, SEQ_PER_GROUP))[None]

    @staticmethod
    def state_to_lanes(s):
        B = s.shape[0]
        s = s.reshape(B // SEQ_PER_GROUP, LANES, RWKV_HEAD, RWKV_HEAD)
        return s.transpose(0, 3, 2, 1)

    @staticmethod
    def state_from_lanes(s):
        G = s.shape[0]
        return s.transpose(0, 3, 2, 1).reshape(G * SEQ_PER_GROUP, RWKV_HEADS, RWKV_HEAD, RWKV_HEAD)


class _SampleLanes:
    @staticmethod
    def to_lanes(x):
        Td, S, _ = x.shape
        x = x.reshape(Td, S // LANES, LANES, RWKV_HEADS, RWKV_HEAD)
        return x.transpose(0, 1, 3, 4, 2).reshape(Td, (S // LANES) * RWKV_HEADS, RWKV_HEAD, LANES)

    @staticmethod
    def from_lanes(o):
        Td, G = o.shape[:2]
        o = o.reshape(Td, G // RWKV_HEADS, RWKV_HEADS, RWKV_HEAD, LANES)
        return o.transpose(0, 1, 4, 2, 3).reshape(Td, (G // RWKV_HEADS) * LANES, RWKV_WIDTH)

    @staticmethod
    def tile(p, n_seq):
        t = jnp.broadcast_to(p.reshape(RWKV_HEADS, RWKV_HEAD, 1), (RWKV_HEADS, RWKV_HEAD, LANES))
        return jnp.tile(t, (n_seq // LANES, 1, 1))

    @staticmethod
    def state_to_lanes(s):
        S = s.shape[0]
        s = s.reshape(S // LANES, LANES, RWKV_HEADS, RWKV_HEAD, RWKV_HEAD)
        return s.transpose(0, 2, 4, 3, 1).reshape((S // LANES) * RWKV_HEADS, RWKV_HEAD, RWKV_HEAD, LANES)

    @staticmethod
    def state_from_lanes(s):
        G = s.shape[0]
        s = s.reshape(G // RWKV_HEADS, RWKV_HEADS, RWKV_HEAD, RWKV_HEAD, LANES)
        return s.transpose(0, 4, 1, 3, 2).reshape((G // RWKV_HEADS) * LANES, RWKV_HEADS, RWKV_HEAD, RWKV_HEAD)


def _rwkv_mix(proj, prev_p, state, lanes, wp):
    n_seq = proj.shape[1]
    r, k, v, w, a, g = _rwkv_pre(proj, prev_p, wp['mu'], wp['wa0'], wp['w_lora'], wp['g2'])
    tiles = [lanes.tile(p, n_seq) for p in wp['head_params']]
    o, s_out = _wkv(*(lanes.to_lanes(t) for t in (r, k, v, w, a)), tiles, lanes.state_to_lanes(state))
    return lanes.from_lanes(o), g, lanes.state_from_lanes(s_out)


def _layer_back(x, mod, o_r, gate, ctx, wp):
    o_m = _uv_norm(ctx, wp['wuv'], wp['out_norm'])
    x1 = _out_proj(o_r, gate, o_m, x, mod, wp['w_out'], wp['ln1_g'], wp['ln1_b'])
    return _ffn(x1, mod, wp['w_up'], wp['w_down'], wp['ln2_g'], wp['ln2_b'])


def kernel(x_prompt, x_sample, c_prompt, c_sample, cache_latent, cache_krope, state_wkv, state_shift, page_table, w_ada, b_ada, w_in, rwkv_mu, rwkv_w0, rwkv_w2, rwkv_a0, rwkv_a2, rwkv_g2, rwkv_k_k, rwkv_k_a, rwkv_r_k, rwkv_lnx_g, rwkv_lnx_b, mla_q_norm, mla_w_q_up, mla_kv_norm, mla_w_uk, mla_w_uv, mla_out_norm, w_out, ln1_g, ln1_b, w_up, w_down, ln2_g, ln2_b):
    B, T, D = x_prompt.shape
    S, Td, _ = x_sample.shape
    past = page_table.shape[1] * cache_latent.shape[2]
    l = 0

    w_lora = jnp.zeros((256, 2 * RWKV_WIDTH), F32)
    w_lora = w_lora.at[:DECAY_LORA, :RWKV_WIDTH].set(rwkv_w2[l])
    w_lora = w_lora.at[DECAY_LORA:DECAY_LORA + AAA_LORA, RWKV_WIDTH:].set(rwkv_a2[l])
    wq = mla_w_q_up[l].reshape(Q_RANK, MLA_HEADS, MLA_NOPE + MLA_ROPE)
    wq_p = jnp.concatenate([wq[:, :, :MLA_NOPE].reshape(Q_RANK, -1), wq[:, :, MLA_NOPE:].reshape(Q_RANK, -1)], -1)
    row = lambda p: p.reshape(1, -1)
    wp = dict(
        w_in=_perm_cols(w_in[l]).astype(BF16),
        mu=row(_perm_cols(rwkv_mu[l])),
        wa0=row(jnp.concatenate([rwkv_w0[l], rwkv_a0[l]])),
        w_lora=w_lora.astype(BF16),
        g2=rwkv_g2[l].astype(BF16),
        head_params=[rwkv_k_k[l], rwkv_k_a[l], rwkv_r_k[l].reshape(-1), rwkv_lnx_g[l], rwkv_lnx_b[l]],
        q_norm=row(mla_q_norm[l]), wq=wq_p.astype(BF16),
        wuk=mla_w_uk[l].transpose(1, 2, 0).astype(BF16),
        kv_norm=row(mla_kv_norm[l]),
        wuv=mla_w_uv[l].transpose(1, 0, 2).astype(BF16),
        out_norm=row(mla_out_norm[l]),
        w_out=w_out[l].astype(BF16), ln1_g=row(ln1_g[l]), ln1_b=row(ln1_b[l]),
        w_up=w_up[l].astype(BF16), w_down=w_down[l].astype(BF16), ln2_g=row(ln2_g[l]), ln2_b=row(ln2_b[l]),
    )

    mod = _ada(jnp.concatenate([c_prompt, c_sample], 0), w_ada[l], b_ada[l])
    mod_p = mod[:B, None, :]
    mod_s = jnp.tile(mod[B:], (Td, 1))[None]

    s0_p = jnp.zeros((B, RWKV_HEADS, RWKV_HEAD, RWKV_HEAD), F32)
    prev0 = jnp.zeros((B, RW_COLS), F32)
    proj_p = _in_proj(x_prompt, mod_p, wp['w_in'])
    o_r, gate_p, wkv_p = _rwkv_mix(proj_p, prev0, s0_p, _PromptLanes, wp)
    cos_t, sin_t = _rope_tables(jnp.arange(T))
    q4, kc, lat_p, kr_p, lat_t = _mla_proj(proj_p, cos_t, sin_t, wp['q_norm'], wp['wq'], wp['wuk'],
                                           wp['kv_norm'])
    last_p = _unperm_shift_cols(proj_p[T - 1, :, :RW_COLS])
    ctx_p = _prompt_attn(q4, kc, lat_t)
    y_p = _layer_back(x_prompt, mod_p, o_r, gate_p, ctx_p, wp)

    xs = x_sample.transpose(1, 0, 2).reshape(1, Td * S, D)
    pos_s = jnp.repeat(past + jnp.arange(Td), S)
    prev_s = _perm_cols(state_shift[l])
    proj = _in_proj(xs, mod_s, wp['w_in']).reshape(Td, S, NP_COLS)
    o_rs, gate_s, wkv_s = _rwkv_mix(proj, prev_s, state_wkv[l], _SampleLanes, wp)
    o_rs = o_rs.reshape(Td * S, 1, RWKV_WIDTH)
    gate_s = gate_s.reshape(Td * S, 1, RWKV_WIDTH)
    cos_t, sin_t = _rope_tables(pos_s)
    q4s, _, lat_s, kr_s, _ = _mla_proj(proj.reshape(Td * S, 1, NP_COLS), cos_t, sin_t, wp['q_norm'], wp['wq'],
                                    wp['wuk'], wp['kv_norm'])
    last_s = _unperm_shift_cols(proj[Td - 1, :, :RW_COLS])
    qk = KV_RANK + MLA_ROPE
    q_s = q4s.reshape(MLA_HEADS, Td, S, qk).transpose(2, 0, 1, 3).reshape(S, MLA_HEADS * Td, qk)
    lat_s = lat_s.reshape(Td, S, KV_RANK).transpose(1, 0, 2)
    kr_s = kr_s.reshape(Td, S, MLA_ROPE).transpose(1, 0, 2)
    pad16 = lambda t: jnp.pad(t, ((0, 0), (0, 16 - Td), (0, 0)))
    ctx_s = _sample_attn(q_s, pad16(lat_s), pad16(kr_s), cache_latent[l], cache_krope[l].transpose(0, 2, 1),
                         page_table)
    ctx_s = ctx_s.reshape(S, MLA_HEADS, Td, KV_RANK).transpose(1, 2, 0, 3).reshape(1, MLA_HEADS, Td * S, KV_RANK)
    y_s = _layer_back(xs, mod_s, o_rs, gate_s, ctx_s, wp)
    y_s = y_s.reshape(Td, S, D).transpose(1, 0, 2)

    return (y_p, y_s, lat_p[None], kr_p[None], wkv_p[None], last_p[None],
            lat_s[None], kr_s[None], wkv_s[None], last_s[None])
```

```python
import functools

import jax
import jax.numpy as jnp
from jax import lax
from jax.experimental import pallas as pl
from jax.experimental.pallas import tpu as pltpu

F32 = jnp.float32
BF16 = jnp.bfloat16

D_MODEL = 2048
RWKV_WIDTH = 1024
RWKV_HEAD = 64
RWKV_HEADS = 16
DECAY_LORA = 96
AAA_LORA = 96
GATE_LORA = 256
MLA_V = 128
MLA_HEADS = 8
MLA_WIDTH = 1024
MLA_NOPE = 128
MLA_ROPE = 64
Q_RANK = 512
KV_RANK = 256
D_FF = 4 * D_MODEL
ROPE_THETA = 10000.0
N_SHIFT = 3 * RWKV_WIDTH + DECAY_LORA + AAA_LORA + GATE_LORA
N_IN = N_SHIFT + Q_RANK + KV_RANK + MLA_ROPE
DEPTH = 1
DEEPNORM_ALPHA = (2.0 * DEPTH) ** 0.25
LN_EPS = 1e-5
RMS_EPS = 1e-6
GN_EPS = 64e-5
ATTN_SCALE = (MLA_NOPE + MLA_ROPE) ** -0.5
NEG_BIG = -1e30

LANES = 128
VMEM_LIMIT = 56 * 1024 * 1024

OFF_R, OFF_K, OFF_V = 0, RWKV_WIDTH, 2 * RWKV_WIDTH
OFF_GD = 3 * RWKV_WIDTH
OFF_WA = OFF_GD + GATE_LORA
RW_COLS = OFF_WA + 256
OFF_CQ = RW_COLS
OFF_CKV = OFF_CQ + Q_RANK
OFF_KR = OFF_CKV + KV_RANK
NP_COLS = 4608
MLA_BLK = 512


def _perm_cols(a):
    z = lambda n: jnp.zeros(a.shape[:-1] + (n,), a.dtype)
    o_wd = 3 * RWKV_WIDTH
    o_gd = o_wd + DECAY_LORA + AAA_LORA
    pieces = [a[..., :o_wd], a[..., o_gd:N_SHIFT], a[..., o_wd:o_gd], z(256 - DECAY_LORA - AAA_LORA)]
    if a.shape[-1] == N_IN:
        pieces += [a[..., N_SHIFT:N_IN], z(NP_COLS - OFF_KR - MLA_ROPE)]
    return jnp.concatenate(pieces, -1)


def _unperm_shift_cols(a):
    return jnp.concatenate([a[..., :OFF_GD], a[..., OFF_WA:OFF_WA + DECAY_LORA + AAA_LORA],
                            a[..., OFF_GD:OFF_WA]], -1)


def _params(**kw):
    return pltpu.CompilerParams(vmem_limit_bytes=VMEM_LIMIT, **kw)


def _sigmoid(x):
    return 1.0 / (1.0 + jnp.exp(-x))


def _layernorm(x, g, b):
    mu = jnp.mean(x, -1, keepdims=True)
    xc = x - mu
    var = jnp.mean(xc * xc, -1, keepdims=True)
    return xc * lax.rsqrt(var + LN_EPS) * g + b


def _rmsnorm(x, g):
    return x * lax.rsqrt(jnp.mean(x * x, -1, keepdims=True) + RMS_EPS) * g


def _ada_body(c_ref, w_ref, b_ref, o_ref):
    c = c_ref[...]
    a = (c * _sigmoid(c)).astype(BF16)
    o_ref[...] = jnp.dot(a, w_ref[...].astype(BF16), preferred_element_type=F32) + b_ref[...]


def _ada(c, w_ada, b_ada):
    rows, d = c.shape
    n = w_ada.shape[1]
    tn = 1024
    return pl.pallas_call(
        _ada_body,
        grid=(n // tn,),
        in_specs=[pl.BlockSpec((rows, d), lambda j: (0, 0)),
                  pl.BlockSpec((d, tn), lambda j: (0, j)),
                  pl.BlockSpec((1, tn), lambda j: (0, j))],
        out_specs=pl.BlockSpec((rows, tn), lambda j: (0, j)),
        out_shape=jax.ShapeDtypeStruct((rows, n), F32),
        compiler_params=_params(dimension_semantics=("parallel",)),
        name="ada_mod",
    )(c, w_ada, b_ada.reshape(1, n))


def _grouped_spec(tm, d, col=0):
    return pl.BlockSpec((None, tm, d), lambda b, r, *_: (b, r, col))


def _mod_spec(mod, tm, chunk):
    if mod.shape[1] == 1:
        return pl.BlockSpec((None, 1, D_MODEL), lambda b, r, *_: (b, 0, chunk))
    return pl.BlockSpec((None, tm, D_MODEL), lambda b, r, *_: (b, r, chunk))


def _const_spec(shape):
    nd = len(shape)
    return pl.BlockSpec(shape, lambda *_: (0,) * nd)


def _inproj_body(x_ref, sc_ref, sh_ref, w_ref, o_ref, h_scr):
    @pl.when(pl.program_id(2) == 0)
    def _():
        h = x_ref[...] * (1.0 + sc_ref[...]) + sh_ref[...]
        h_scr[...] = h.astype(BF16)

    o_ref[...] = jnp.dot(h_scr[...], w_ref[...], preferred_element_type=F32)


def _in_proj(x, mod, w_in_p):
    B, T, D = x.shape
    tm = min(512, T)
    nj = 2
    tn = NP_COLS // nj
    return pl.pallas_call(
        _inproj_body,
        grid=(B, T // tm, nj),
        in_specs=[_grouped_spec(tm, D), _mod_spec(mod, tm, 1), _mod_spec(mod, tm, 0),
                  pl.BlockSpec((D, tn), lambda b, r, j: (0, j))],
        out_specs=pl.BlockSpec((None, tm, tn), lambda b, r, j: (b, r, j)),
        out_shape=jax.ShapeDtypeStruct((B, T, NP_COLS), F32),
        scratch_shapes=[pltpu.VMEM((tm, D), BF16)],
        compiler_params=_params(dimension_semantics=("parallel", "parallel", "arbitrary")),
        name="in_proj",
    )(x, mod, mod, w_in_p)


SUBLANES = 8


def _rwkv_pre_body(p_ref, pb_ref, prev_ref, mu_ref, wa0_ref, wl_ref, g2_ref,
                   r_ref, k_ref, v_ref, w_ref, a_ref, g_ref, *, shift):
    p = p_ref[...]
    at_start = pl.program_id(1) == 0
    if shift == 1:
        first = jnp.where(at_start, prev_ref[...], pb_ref[SUBLANES - 1:SUBLANES, :])
        row = lax.broadcasted_iota(jnp.int32, p.shape, 0)
        shifted = jnp.where(row == 0, first, pltpu.roll(p, 1, axis=0))
    else:
        first = jnp.where(at_start, prev_ref[...], pb_ref[...])
        shifted = jnp.concatenate([first, p[:-shift]], axis=0)
    z = p + mu_ref[...] * (shifted - p)
    r_ref[...] = z[:, OFF_R:OFF_R + RWKV_WIDTH]
    k_ref[...] = z[:, OFF_K:OFF_K + RWKV_WIDTH]
    v_ref[...] = z[:, OFF_V:OFF_V + RWKV_WIDTH]
    wa = z[:, OFF_WA:OFF_WA + 256]
    lane = lax.broadcasted_iota(jnp.int32, wa.shape, 1)
    lora_in = jnp.where(lane < DECAY_LORA, jnp.tanh(wa), wa).astype(BF16)
    pre = jnp.dot(lora_in, wl_ref[...], preferred_element_type=F32) + wa0_ref[...]
    u = pre[:, :RWKV_WIDTH]
    w_log = jnp.minimum(u, 0.0) - jnp.log1p(jnp.exp(-jnp.abs(u))) - 0.5
    w_ref[...] = jnp.exp(-jnp.exp(w_log))
    a_ref[...] = _sigmoid(pre[:, RWKV_WIDTH:])
    gd = _sigmoid(z[:, OFF_GD:OFF_GD + GATE_LORA]).astype(BF16)
    g_ref[...] = jnp.dot(gd, g2_ref[...], preferred_element_type=F32)


def _rwkv_pre(proj, prev_p, shift, mu_p, wa0, w_lora, g2):
    B, T, _ = proj.shape
    tm = min(256, T)
    back = max(shift, SUBLANES)
    assert tm % back == 0 and prev_p.shape[1] == shift
    blk = _grouped_spec(tm, RWKV_WIDTH)
    out = jax.ShapeDtypeStruct((B, T, RWKV_WIDTH), F32)
    return pl.pallas_call(
        functools.partial(_rwkv_pre_body, shift=shift),
        grid=(B, T // tm),
        in_specs=[_grouped_spec(tm, RW_COLS),
                  pl.BlockSpec((None, back, RW_COLS), lambda b, r: (b, jnp.maximum(r * (tm // back) - 1, 0), 0)),
                  pl.BlockSpec((None, shift, RW_COLS), lambda b, r: (b, 0, 0)),
                  _const_spec((1, RW_COLS)), _const_spec((1, 2 * RWKV_WIDTH)),
                  _const_spec((256, 2 * RWKV_WIDTH)), _const_spec((GATE_LORA, RWKV_WIDTH))],
        out_specs=[blk] * 6,
        out_shape=[out] * 6,
        compiler_params=_params(dimension_semantics=("parallel", "parallel")),
        name="rwkv_pre",
    )(proj, proj, prev_p, mu_p, wa0, w_lora, g2)


WKV_UNROLL = 8


def _wkv_body(r_ref, k_ref, v_ref, w_ref, a_ref, kk_p, ka_p, rk_p, lg_p, lb_p, s0_ref,
              o_ref, sout_ref, S, b_kk, b_wr, b_w, b_kka, b_k2):
    tt = r_ref.shape[0]
    N = RWKV_HEAD

    @pl.when(pl.program_id(1) == 0)
    def _():
        S[...] = s0_ref[...]

    def tok(t, carry):
        rT = r_ref[t]
        kT = k_ref[t]
        aT = a_ref[t]
        wT = w_ref[t]
        kk = kT * kk_p[...]
        nrm = jnp.sqrt(jnp.sum(kk * kk, axis=0, keepdims=True))
        kk = kk / jnp.maximum(nrm, 1e-12)
        k2 = kT * (1.0 + (aT - 1.0) * ka_p[...])
        kka = kk * aT
        b_kk[...] = kk
        b_wr[...] = wT * rT
        b_w[...] = wT
        b_kka[...] = kka
        b_k2[...] = k2
        c1 = jnp.sum(kka * rT, axis=0, keepdims=True)
        c2 = jnp.sum(k2 * rT, axis=0, keepdims=True)
        bonus = jnp.sum(rT * k2 * rk_p[...], axis=0, keepdims=True)

        def reduce_pass(j0, acc):
            sa, o1 = acc
            for jj in range(WKV_UNROLL):
                j = j0 * WKV_UNROLL + jj
                Sj = S[j]
                sa = sa + Sj * b_kk[pl.ds(j, 1), :]
                o1 = o1 + Sj * b_wr[pl.ds(j, 1), :]
            return sa, o1

        zero = jnp.zeros((N, LANES), F32)
        sa, o1 = lax.fori_loop(0, N // WKV_UNROLL, reduce_pass, (zero, zero))
        vT = v_ref[t]
        o = o1 - sa * c1 + vT * c2

        def update_pass(j0, c):
            for jj in range(WKV_UNROLL):
                j = j0 * WKV_UNROLL + jj
                S[j] = S[j] * b_w[pl.ds(j, 1), :] - sa * b_kka[pl.ds(j, 1), :] + vT * b_k2[pl.ds(j, 1), :]
            return c

        lax.fori_loop(0, N // WKV_UNROLL, update_pass, 0)

        mu = jnp.mean(o, axis=0, keepdims=True)
        oc = o - mu
        var = jnp.mean(oc * oc, axis=0, keepdims=True)
        on = oc * lax.rsqrt(var + GN_EPS) * lg_p[...] + lb_p[...]
        o_ref[t] = on + bonus * vT
        return carry

    lax.fori_loop(0, tt, tok, 0)

    @pl.when(pl.program_id(1) == pl.num_programs(1) - 1)
    def _():
        sout_ref[...] = S[...]


def _wkv(r, k, v, w, a, tiles, s0):
    T, G, N, _ = r.shape
    tt = min(32, T)
    tok_spec = pl.BlockSpec((tt, None, N, LANES), lambda g_, i: (i, g_, 0, 0))
    st_spec = pl.BlockSpec((None, N, N, LANES), lambda g_, i: (g_, 0, 0, 0))
    per_group = tiles[0].shape[0] > 1
    tile_spec = pl.BlockSpec((None, N, LANES), lambda g_, i: (g_ if per_group else 0, 0, 0))
    return pl.pallas_call(
        _wkv_body,
        grid=(G, T // tt),
        in_specs=[tok_spec] * 5 + [tile_spec] * 5 + [st_spec],
        out_specs=[tok_spec, st_spec],
        out_shape=[jax.ShapeDtypeStruct((T, G, N, LANES), F32), jax.ShapeDtypeStruct((G, N, N, LANES), F32)],
        scratch_shapes=[pltpu.VMEM((N, N, LANES), F32)] + [pltpu.VMEM((N, LANES), F32)] * 5,
        compiler_params=_params(dimension_semantics=("parallel", "arbitrary")),
        name="wkv_scan",
    )(r, k, v, w, a, *tiles, s0)


def _rot_half(x):
    n = x.shape[-1]
    lane = lax.broadcasted_iota(jnp.int32, x.shape, x.ndim - 1)
    fwd = pltpu.roll(x, n - MLA_ROPE // 2, axis=x.ndim - 1)
    bwd = pltpu.roll(x, MLA_ROPE // 2, axis=x.ndim - 1)
    return jnp.where(lane % MLA_ROPE < MLA_ROPE // 2, fwd, bwd)


def _mla_body(cq_ref, ckr_ref, cos_ref, sin_ref, qn_ref, wq_ref, wuk_ref, kvn_ref,
              q_ref, kc_ref, lat_ref, kr_ref, latt_ref):
    cq = cq_ref[...]
    q = jnp.dot(_rmsnorm(cq, qn_ref[...]).astype(BF16), wq_ref[...], preferred_element_type=F32)
    nope_w = MLA_HEADS * MLA_NOPE
    qr = q[:, nope_w:]
    q_rope = qr * cos_ref[...] + _rot_half(qr) * sin_ref[...]
    for h in range(MLA_HEADS):
        qn = q[:, h * MLA_NOPE:(h + 1) * MLA_NOPE].astype(BF16)
        q_ref[h, :, :KV_RANK] = jnp.dot(qn, wuk_ref[h], preferred_element_type=F32).astype(BF16)
        q_ref[h, :, KV_RANK:] = q_rope[:, h * MLA_ROPE:(h + 1) * MLA_ROPE].astype(BF16)
    ckr = ckr_ref[...]
    lat = _rmsnorm(ckr[:, :KV_RANK], kvn_ref[...])
    slab = ckr[:, KV_RANK:KV_RANK + LANES]
    kr = (slab * cos_ref[:, :LANES] + _rot_half(slab) * sin_ref[:, :LANES])[:, :MLA_ROPE]
    lat_ref[...] = lat
    kr_ref[...] = kr
    kc_ref[:, :KV_RANK] = lat.astype(BF16)
    kc_ref[:, KV_RANK:] = kr.astype(BF16)
    latt_ref[...] = lat.T.astype(BF16)


def _mla_proj(proj, cos_t, sin_t, q_norm, wq_p, wuk_t, kv_norm):
    B, T, _ = proj.shape
    tm = min(512, T)
    qk = KV_RANK + MLA_ROPE
    tab = pl.BlockSpec((None, tm, MLA_HEADS * MLA_ROPE), lambda b, r: (0, r, 0))
    return pl.pallas_call(
        _mla_body,
        grid=(B, T // tm),
        in_specs=[_grouped_spec(tm, MLA_BLK, OFF_CQ // MLA_BLK), _grouped_spec(tm, MLA_BLK, OFF_CKV // MLA_BLK),
                  tab, tab, _const_spec((1, Q_RANK)), _const_spec(wq_p.shape), _const_spec(wuk_t.shape),
                  _const_spec((1, KV_RANK))],
        out_specs=[pl.BlockSpec((None, MLA_HEADS, tm, qk), lambda b, r: (b, 0, r, 0)),
                   _grouped_spec(tm, qk), _grouped_spec(tm, KV_RANK), _grouped_spec(tm, MLA_ROPE),
                   pl.BlockSpec((None, KV_RANK, tm), lambda b, r: (b, 0, r))],
        out_shape=[jax.ShapeDtypeStruct((B, MLA_HEADS, T, qk), BF16),
                   jax.ShapeDtypeStruct((B, T, qk), BF16),
                   jax.ShapeDtypeStruct((B, T, KV_RANK), F32),
                   jax.ShapeDtypeStruct((B, T, MLA_ROPE), F32),
                   jax.ShapeDtypeStruct((B, KV_RANK, T), BF16)],
        compiler_params=_params(dimension_semantics=("parallel", "parallel")),
        name="mla_proj",
    )(proj, proj, cos_t, sin_t, q_norm, wq_p, wuk_t, kv_norm)


HEADS_PER_DOT = 4


def _pattn_body(q_ref, k_ref, vt_ref, o_ref, m_scr, l_scr, acc_scr, *, tq, tk):
    i = pl.program_id(1)
    j = pl.program_id(2)
    last_j = ((i + 1) * tq - 1) // tk

    @pl.when(j == 0)
    def _():
        m_scr[...] = jnp.full(m_scr.shape, NEG_BIG, F32)
        l_scr[...] = jnp.zeros(l_scr.shape, F32)
        acc_scr[...] = jnp.zeros(acc_scr.shape, F32)

    def step(masked):
        k = k_ref[...]
        vt = vt_ref[...]
        if masked:
            cols = HEADS_PER_DOT * tq
            kpos = j * tk + lax.broadcasted_iota(jnp.int32, (tk, cols), 0)
            qpos = i * tq + lax.broadcasted_iota(jnp.int32, (tk, cols), 1) % tq
            keep = kpos <= qpos
        for g in range(MLA_HEADS // HEADS_PER_DOT):
            q = q_ref[g * HEADS_PER_DOT:(g + 1) * HEADS_PER_DOT].reshape(HEADS_PER_DOT * tq, q_ref.shape[-1])
            st = lax.dot_general(k, q, (((1,), (1,)), ((), ())),
                                 preferred_element_type=F32) * ATTN_SCALE
            if masked:
                st = jnp.where(keep, st, NEG_BIG)
            m_prev = m_scr[g]
            m_new = jnp.maximum(m_prev, jnp.max(st, axis=0, keepdims=True))
            alpha = jnp.exp(m_prev - m_new)
            pt = jnp.exp(st - m_new)
            l_scr[g] = alpha * l_scr[g] + jnp.sum(pt, axis=0, keepdims=True)
            acc_scr[g] = alpha * acc_scr[g] + jnp.dot(vt, pt.astype(BF16), preferred_element_type=F32)
            m_scr[g] = m_new

    crosses_diagonal = (j + 1) * tk - 1 > i * tq
    pl.when(jnp.logical_and(j <= last_j, crosses_diagonal))(functools.partial(step, True))
    pl.when(jnp.logical_and(j <= last_j, jnp.logical_not(crosses_diagonal)))(functools.partial(step, False))

    @pl.when(j == pl.num_programs(2) - 1)
    def _():
        for g in range(MLA_HEADS // HEADS_PER_DOT):
            ctx_t = acc_scr[g] / l_scr[g]
            for hh in range(HEADS_PER_DOT):
                o_ref[g * HEADS_PER_DOT + hh] = ctx_t[:, hh * tq:(hh + 1) * tq].T.astype(BF16)


def _prompt_attn(q4, kc, lat_t):
    B, H, T, qk = q4.shape
    tq = min(256, T)
    tk = min(512, T)
    body = functools.partial(_pattn_body, tq=tq, tk=tk)
    kv_block = lambda i, j: jnp.minimum(j, ((i + 1) * tq - 1) // tk)
    return pl.pallas_call(
        body,
        grid=(B, T // tq, T // tk),
        in_specs=[pl.BlockSpec((None, H, tq, qk), lambda b, i, j: (b, 0, i, 0)),
                  pl.BlockSpec((None, tk, qk), lambda b, i, j: (b, kv_block(i, j), 0)),
                  pl.BlockSpec((None, KV_RANK, tk), lambda b, i, j: (b, 0, kv_block(i, j)))],
        out_specs=pl.BlockSpec((None, H, tq, KV_RANK), lambda b, i, j: (b, 0, i, 0)),
        out_shape=jax.ShapeDtypeStruct((B, H, T, KV_RANK), BF16),
        scratch_shapes=[pltpu.VMEM((H // HEADS_PER_DOT, 1, HEADS_PER_DOT * tq), F32),
                        pltpu.VMEM((H // HEADS_PER_DOT, 1, HEADS_PER_DOT * tq), F32),
                        pltpu.VMEM((H // HEADS_PER_DOT, KV_RANK, HEADS_PER_DOT * tq), F32)],
        compiler_params=_params(dimension_semantics=("parallel", "parallel", "arbitrary")),
        name="prompt_attn",
    )(q4, kc, lat_t)


SAMPLE_PAGES_PER_STEP = 64


def _sattn_body(pt_ref, q_ref, latn_ref, krn_ref, *rest, n_pg, n_new):
    lat_refs = rest[:n_pg]
    kr_refs = rest[n_pg:2 * n_pg]
    o_ref, m_scr, l_scr, acc_scr = rest[2 * n_pg:]
    c = pl.program_id(1)
    nt = (((1,), (1,)), ((), ()))

    @pl.when(c == 0)
    def _():
        m_scr[...] = jnp.full(m_scr.shape, NEG_BIG, F32)
        l_scr[...] = jnp.zeros(l_scr.shape, F32)
        acc_scr[...] = jnp.zeros(acc_scr.shape, F32)

    q = q_ref[...]
    q_lat = q[:, :KV_RANK]
    q_rope = q[:, KV_RANK:]

    def update(s, vals):
        m_prev = m_scr[...]
        m_new = jnp.maximum(m_prev, jnp.max(s, axis=1, keepdims=True))
        alpha = jnp.exp(m_prev - m_new)
        p = jnp.exp(s - m_new)
        l_scr[...] = alpha * l_scr[...] + jnp.sum(p, axis=1, keepdims=True)
        acc_scr[...] = alpha * acc_scr[...] + jnp.dot(p.astype(BF16), vals, preferred_element_type=F32)
        m_scr[...] = m_new

    lat_all = jnp.concatenate([lat_refs[p][...].astype(BF16) for p in range(n_pg)], axis=0)
    krt_all = jnp.concatenate([kr_refs[p][...].astype(BF16) for p in range(n_pg)], axis=1)
    s = lax.dot_general(q_lat, lat_all, nt, preferred_element_type=F32)
    s = (s + jnp.dot(q_rope, krt_all, preferred_element_type=F32)) * ATTN_SCALE
    update(s, lat_all)

    @pl.when(c == pl.num_programs(1) - 1)
    def _():
        latn = latn_ref[...].astype(BF16)
        sn = lax.dot_general(q_lat, latn, nt, preferred_element_type=F32)
        sn = (sn + lax.dot_general(q_rope, krn_ref[...].astype(BF16), nt, preferred_element_type=F32)) * ATTN_SCALE
        row_t = lax.broadcasted_iota(jnp.int32, sn.shape, 0) % n_new
        col_t = lax.broadcasted_iota(jnp.int32, sn.shape, 1)
        sn = jnp.where(col_t <= row_t, sn, NEG_BIG)
        update(sn, latn)
        o_ref[...] = acc_scr[...] / l_scr[...]


def _sample_attn(q, latn, krn, cache_lat, cache_kr, page_table):
    S, rows, qk = q.shape
    n_pages = page_table.shape[1]
    page = cache_lat.shape[1]
    n_pg = min(SAMPLE_PAGES_PER_STEP, n_pages)
    n_new = rows // MLA_HEADS
    pad_new = latn.shape[1]
    lat_specs = [pl.BlockSpec((None, page, KV_RANK),
                              lambda s, c, pt, p=p: (pt[s, c * n_pg + p], 0, 0)) for p in range(n_pg)]
    kr_specs = [pl.BlockSpec((None, MLA_ROPE, page),
                             lambda s, c, pt, p=p: (pt[s, c * n_pg + p], 0, 0)) for p in range(n_pg)]
    grid_spec = pltpu.PrefetchScalarGridSpec(
        num_scalar_prefetch=1,
        grid=(S, n_pages // n_pg),
        in_specs=[pl.BlockSpec((None, rows, qk), lambda s, c, pt: (s, 0, 0)),
                  pl.BlockSpec((None, pad_new, KV_RANK), lambda s, c, pt: (s, 0, 0)),
                  pl.BlockSpec((None, pad_new, MLA_ROPE), lambda s, c, pt: (s, 0, 0))] + lat_specs + kr_specs,
        out_specs=pl.BlockSpec((None, rows, KV_RANK), lambda s, c, pt: (s, 0, 0)),
        scratch_shapes=[pltpu.VMEM((rows, 1), F32), pltpu.VMEM((rows, 1), F32),
                        pltpu.VMEM((rows, KV_RANK), F32)],
    )
    return pl.pallas_call(
        functools.partial(_sattn_body, n_pg=n_pg, n_new=n_new),
        grid_spec=grid_spec,
        out_shape=jax.ShapeDtypeStruct((S, rows, KV_RANK), F32),
        compiler_params=_params(dimension_semantics=("parallel", "arbitrary")),
        name="sample_attn",
    )(page_table, q, latn, krn, *([cache_lat] * n_pg), *([cache_kr] * n_pg))


def _uv_body(ctx_ref, wuv_ref, on_ref, o_ref):
    parts = [jnp.dot(ctx_ref[h].astype(BF16), wuv_ref[h], preferred_element_type=F32)
             for h in range(MLA_HEADS)]
    om = jnp.concatenate(parts, axis=1)
    o_ref[...] = _rmsnorm(om, on_ref[...]).astype(BF16)


def _uv_norm(ctx, wuv_t, out_norm):
    B, H, T, R = ctx.shape
    tm = min(512, T)
    return pl.pallas_call(
        _uv_body,
        grid=(B, T // tm),
        in_specs=[pl.BlockSpec((None, H, tm, R), lambda b, r: (b, 0, r, 0)),
                  _const_spec(wuv_t.shape), _const_spec((1, MLA_WIDTH))],
        out_specs=_grouped_spec(tm, MLA_WIDTH),
        out_shape=jax.ShapeDtypeStruct((B, T, MLA_WIDTH), BF16),
        compiler_params=_params(dimension_semantics=("parallel", "parallel")),
        name="uv_norm",
    )(ctx, wuv_t, out_norm)


def _outproj_body(or_ref, gate_ref, om_ref, x_ref, g1_ref, w_ref, lg_ref, lb_ref, o_ref):
    o_r = (or_ref[...] * gate_ref[...]).astype(BF16)
    mixed = jnp.dot(o_r, w_ref[:RWKV_WIDTH, :], preferred_element_type=F32)
    mixed = mixed + jnp.dot(om_ref[...], w_ref[RWKV_WIDTH:, :], preferred_element_type=F32)
    y = DEEPNORM_ALPHA * x_ref[...] + g1_ref[...] * mixed
    o_ref[...] = _layernorm(y, lg_ref[...], lb_ref[...])


def _out_proj(o_r, gate, o_m, x, mod, w_out, ln_g, ln_b):
    B, T, D = x.shape
    tm = min(256, T)
    rw = _grouped_spec(tm, RWKV_WIDTH)
    return pl.pallas_call(
        _outproj_body,
        grid=(B, T // tm),
        in_specs=[rw, rw, _grouped_spec(tm, MLA_WIDTH), _grouped_spec(tm, D),
                  _mod_spec(mod, tm, 2), _const_spec((D, D)), _const_spec((1, D)), _const_spec((1, D))],
        out_specs=_grouped_spec(tm, D),
        out_shape=jax.ShapeDtypeStruct((B, T, D), F32),
        compiler_params=_params(dimension_semantics=("parallel", "parallel")),
        name="out_proj_ln1",
    )(o_r, gate, o_m, x, mod, w_out, ln_g, ln_b)


def _ffn_body(x_ref, sc_ref, sh_ref, g2_ref, wu_ref, wd_ref, lg_ref, lb_ref, o_ref, h_scr, acc_scr):
    f = pl.program_id(2)

    @pl.when(f == 0)
    def _():
        h_scr[...] = (x_ref[...] * (1.0 + sc_ref[...]) + sh_ref[...]).astype(BF16)
        acc_scr[...] = jnp.zeros(acc_scr.shape, F32)

    u = jnp.maximum(jnp.dot(h_scr[...], wu_ref[...], preferred_element_type=F32), 0.0)
    acc_scr[...] += jnp.dot((u * u).astype(BF16), wd_ref[...], preferred_element_type=F32)

    @pl.when(f == pl.num_programs(2) - 1)
    def _():
        y = DEEPNORM_ALPHA * x_ref[...] + g2_ref[...] * acc_scr[...]
        o_ref[...] = _layernorm(y, lg_ref[...], lb_ref[...])


def _ffn(x, mod, w_up, w_down, ln_g, ln_b):
    B, T, D = x.shape
    tm = min(512, T)
    tf = 1024
    return pl.pallas_call(
        _ffn_body,
        grid=(B, T // tm, D_FF // tf),
        in_specs=[_grouped_spec(tm, D), _mod_spec(mod, tm, 4), _mod_spec(mod, tm, 3), _mod_spec(mod, tm, 5),
                  pl.BlockSpec((D, tf), lambda b, r, f: (0, f)),
                  pl.BlockSpec((tf, D), lambda b, r, f: (f, 0)),
                  _const_spec((1, D)), _const_spec((1, D))],
        out_specs=_grouped_spec(tm, D),
        out_shape=jax.ShapeDtypeStruct((B, T, D), F32),
        scratch_shapes=[pltpu.VMEM((tm, D), BF16), pltpu.VMEM((tm, D), F32)],
        compiler_params=_params(dimension_semantics=("parallel", "parallel", "arbitrary")),
        name="ffn_ln2",
    )(x, mod, mod, mod, w_up, w_down, ln_g, ln_b)


def _rope_tables(pos):
    half = MLA_ROPE // 2
    inv = ROPE_THETA ** (-jnp.arange(half, dtype=F32) / half)
    ang = pos.astype(F32)[:, None] * inv
    cos, sin = jnp.cos(ang), jnp.sin(ang)
    cos_t = jnp.tile(jnp.concatenate([cos, cos], -1), (1, MLA_HEADS))
    sin_t = jnp.tile(jnp.concatenate([-sin, sin], -1), (1, MLA_HEADS))
    return cos_t[None], sin_t[None]


SEQ_PER_GROUP = LANES // RWKV_HEADS


class _PromptLanes:
    @staticmethod
    def to_lanes(x, n_seq):
        B, T, _ = x.shape
        x = x.reshape(B // SEQ_PER_GROUP, SEQ_PER_GROUP, T, RWKV_HEADS, RWKV_HEAD)
        return x.transpose(2, 0, 4, 1, 3).reshape(T, B // SEQ_PER_GROUP, RWKV_HEAD, LANES)

    @staticmethod
    def from_lanes(o):
        T, G = o.shape[:2]
        o = o.reshape(T, G, RWKV_HEAD, SEQ_PER_GROUP, RWKV_HEADS)
        return o.transpose(1, 3, 0, 4, 2).reshape(G * SEQ_PER_GROUP, T, RWKV_WIDTH)

    @staticmethod
    def tile(p, n_seq):
        return jnp.tile(p.reshape(RWKV_HEADS, RWKV_HEAD).T, (1, SEQ_PER_GROUP))[None]

    @staticmethod
    def state_to_lanes(s):
        B = s.shape[0]
        s = s.reshape(B // SEQ_PER_GROUP, LANES, RWKV_HEAD, RWKV_HEAD)
        return s.transpose(0, 3, 2, 1)

    @staticmethod
    def state_from_lanes(s):
        G = s.shape[0]
        return s.transpose(0, 3, 2, 1).reshape(G * SEQ_PER_GROUP, RWKV_HEADS, RWKV_HEAD, RWKV_HEAD)


class _SampleLanes:
    @staticmethod
    def to_lanes(x, n_seq):
        x = x.reshape(-1, n_seq // LANES, LANES, RWKV_HEADS, RWKV_HEAD)
        return x.transpose(0, 1, 3, 4, 2).reshape(x.shape[0], (n_seq // LANES) * RWKV_HEADS, RWKV_HEAD, LANES)

    @staticmethod
    def from_lanes(o):
        Td, G = o.shape[:2]
        o = o.reshape(Td, G // RWKV_HEADS, RWKV_HEADS, RWKV_HEAD, LANES)
        return o.transpose(0, 1, 4, 2, 3).reshape(1, Td * (G // RWKV_HEADS) * LANES, RWKV_WIDTH)

    @staticmethod
    def tile(p, n_seq):
        t = jnp.broadcast_to(p.reshape(RWKV_HEADS, RWKV_HEAD, 1), (RWKV_HEADS, RWKV_HEAD, LANES))
        return jnp.tile(t, (n_seq // LANES, 1, 1))

    @staticmethod
    def state_to_lanes(s):
        S = s.shape[0]
        s = s.reshape(S // LANES, LANES, RWKV_HEADS, RWKV_HEAD, RWKV_HEAD)
        return s.transpose(0, 2, 4, 3, 1).reshape((S // LANES) * RWKV_HEADS, RWKV_HEAD, RWKV_HEAD, LANES)

    @staticmethod
    def state_from_lanes(s):
        G = s.shape[0]
        s = s.reshape(G // RWKV_HEADS, RWKV_HEADS, RWKV_HEAD, RWKV_HEAD, LANES)
        return s.transpose(0, 4, 1, 3, 2).reshape((G // RWKV_HEADS) * LANES, RWKV_HEADS, RWKV_HEAD, RWKV_HEAD)


def _rwkv_mix(proj, prev_p, state, lanes, wp):
    n_seq = state.shape[0]
    r, k, v, w, a, g = _rwkv_pre(proj, prev_p, prev_p.shape[1], wp['mu'], wp['wa0'], wp['w_lora'], wp['g2'])
    tiles = [lanes.tile(p, n_seq) for p in wp['head_params']]
    o, s_out = _wkv(*(lanes.to_lanes(t, n_seq) for t in (r, k, v, w, a)), tiles, lanes.state_to_lanes(state))
    return lanes.from_lanes(o), g, lanes.state_from_lanes(s_out)


def _layer_back(x, mod, o_r, gate, ctx, wp):
    o_m = _uv_norm(ctx, wp['wuv'], wp['out_norm'])
    x1 = _out_proj(o_r, gate, o_m, x, mod, wp['w_out'], wp['ln1_g'], wp['ln1_b'])
    return _ffn(x1, mod, wp['w_up'], wp['w_down'], wp['ln2_g'], wp['ln2_b'])


def kernel(x_prompt, x_sample, c_prompt, c_sample, cache_latent, cache_krope, state_wkv, state_shift, page_table, w_ada, b_ada, w_in, rwkv_mu, rwkv_w0, rwkv_w2, rwkv_a0, rwkv_a2, rwkv_g2, rwkv_k_k, rwkv_k_a, rwkv_r_k, rwkv_lnx_g, rwkv_lnx_b, mla_q_norm, mla_w_q_up, mla_kv_norm, mla_w_uk, mla_w_uv, mla_out_norm, w_out, ln1_g, ln1_b, w_up, w_down, ln2_g, ln2_b):
    B, T, D = x_prompt.shape
    S, Td, _ = x_sample.shape
    past = page_table.shape[1] * cache_latent.shape[2]
    l = 0

    w_lora = jnp.zeros((256, 2 * RWKV_WIDTH), F32)
    w_lora = w_lora.at[:DECAY_LORA, :RWKV_WIDTH].set(rwkv_w2[l])
    w_lora = w_lora.at[DECAY_LORA:DECAY_LORA + AAA_LORA, RWKV_WIDTH:].set(rwkv_a2[l])
    wq = mla_w_q_up[l].reshape(Q_RANK, MLA_HEADS, MLA_NOPE + MLA_ROPE)
    wq_p = jnp.concatenate([wq[:, :, :MLA_NOPE].reshape(Q_RANK, -1), wq[:, :, MLA_NOPE:].reshape(Q_RANK, -1)], -1)
    row = lambda p: p.reshape(1, -1)
    wp = dict(
        w_in=_perm_cols(w_in[l]).astype(BF16),
        mu=row(_perm_cols(rwkv_mu[l])),
        wa0=row(jnp.concatenate([rwkv_w0[l], rwkv_a0[l]])),
        w_lora=w_lora.astype(BF16),
        g2=rwkv_g2[l].astype(BF16),
        head_params=[rwkv_k_k[l], rwkv_k_a[l], rwkv_r_k[l].reshape(-1), rwkv_lnx_g[l], rwkv_lnx_b[l]],
        q_norm=row(mla_q_norm[l]), wq=wq_p.astype(BF16),
        wuk=mla_w_uk[l].transpose(1, 2, 0).astype(BF16),
        kv_norm=row(mla_kv_norm[l]),
        wuv=mla_w_uv[l].transpose(1, 0, 2).astype(BF16),
        out_norm=row(mla_out_norm[l]),
        w_out=w_out[l].astype(BF16), ln1_g=row(ln1_g[l]), ln1_b=row(ln1_b[l]),
        w_up=w_up[l].astype(BF16), w_down=w_down[l].astype(BF16), ln2_g=row(ln2_g[l]), ln2_b=row(ln2_b[l]),
    )

    mod = _ada(jnp.concatenate([c_prompt, c_sample], 0), w_ada[l], b_ada[l])
    mod_p = mod[:B, None, :]
    mod_s = jnp.tile(mod[B:], (Td, 1))[None]

    s0_p = jnp.zeros((B, RWKV_HEADS, RWKV_HEAD, RWKV_HEAD), F32)
    prev0 = jnp.zeros((B, 1, RW_COLS), F32)
    proj_p = _in_proj(x_prompt, mod_p, wp['w_in'])
    o_r, gate_p, wkv_p = _rwkv_mix(proj_p, prev0, s0_p, _PromptLanes, wp)
    cos_t, sin_t = _rope_tables(jnp.arange(T))
    q4, kc, lat_p, kr_p, lat_t = _mla_proj(proj_p, cos_t, sin_t, wp['q_norm'], wp['wq'], wp['wuk'],
                                           wp['kv_norm'])
    last_p = _unperm_shift_cols(proj_p[:, T - 1, :RW_COLS])
    ctx_p = _prompt_attn(q4, kc, lat_t)
    y_p = _layer_back(x_prompt, mod_p, o_r, gate_p, ctx_p, wp)

    xs = x_sample.transpose(1, 0, 2).reshape(1, Td * S, D)
    pos_s = jnp.repeat(past + jnp.arange(Td), S)
    prev_s = _perm_cols(state_shift[l])[None]
    proj = _in_proj(xs, mod_s, wp['w_in'])
    o_rs, gate_s, wkv_s = _rwkv_mix(proj, prev_s, state_wkv[l], _SampleLanes, wp)
    cos_t, sin_t = _rope_tables(pos_s)
    q4s, _, lat_s, kr_s, _ = _mla_proj(proj, cos_t, sin_t, wp['q_norm'], wp['wq'], wp['wuk'], wp['kv_norm'])
    last_s = _unperm_shift_cols(proj[0, (Td - 1) * S:, :RW_COLS])
    qk = KV_RANK + MLA_ROPE
    q_s = q4s.reshape(MLA_HEADS, Td, S, qk).transpose(2, 0, 1, 3).reshape(S, MLA_HEADS * Td, qk)
    lat_s = lat_s.reshape(Td, S, KV_RANK).transpose(1, 0, 2)
    kr_s = kr_s.reshape(Td, S, MLA_ROPE).transpose(1, 0, 2)
    pad16 = lambda t: jnp.pad(t, ((0, 0), (0, 16 - Td), (0, 0)))
    ctx_s = _sample_attn(q_s, pad16(lat_s), pad16(kr_s), cache_latent[l], cache_krope[l].transpose(0, 2, 1),
                         page_table)
    ctx_s = ctx_s.reshape(S, MLA_HEADS, Td, KV_RANK).transpose(1, 2, 0, 3).reshape(1, MLA_HEADS, Td * S, KV_RANK)
    y_s = _layer_back(xs, mod_s, o_rs, gate_s, ctx_s, wp)
    y_s = y_s.reshape(Td, S, D).transpose(1, 0, 2)

    return (y_p, y_s, lat_p[None], kr_p[None], wkv_p[None], last_p[None],
            lat_s[None], kr_s[None], wkv_s[None], last_s[None])
```

```python
import functools

import jax
import jax.numpy as jnp
from jax import lax
from jax.experimental import pallas as pl
from jax.experimental.pallas import tpu as pltpu

F32 = jnp.float32
BF16 = jnp.bfloat16

D_MODEL = 2048
RWKV_WIDTH = 1024
RWKV_HEAD = 64
RWKV_HEADS = 16
DECAY_LORA = 96
AAA_LORA = 96
GATE_LORA = 256
MLA_V = 128
MLA_HEADS = 8
MLA_WIDTH = 1024
MLA_NOPE = 128
MLA_ROPE = 64
Q_RANK = 512
KV_RANK = 256
D_FF = 4 * D_MODEL
ROPE_THETA = 10000.0
N_SHIFT = 3 * RWKV_WIDTH + DECAY_LORA + AAA_LORA + GATE_LORA
N_IN = N_SHIFT + Q_RANK + KV_RANK + MLA_ROPE
DEPTH = 1
DEEPNORM_ALPHA = (2.0 * DEPTH) ** 0.25
LN_EPS = 1e-5
RMS_EPS = 1e-6
GN_EPS = 64e-5
ATTN_SCALE = (MLA_NOPE + MLA_ROPE) ** -0.5
NEG_BIG = -1e30

LANES = 128
VMEM_LIMIT = 56 * 1024 * 1024

OFF_R, OFF_K, OFF_V = 0, RWKV_WIDTH, 2 * RWKV_WIDTH
OFF_GD = 3 * RWKV_WIDTH
OFF_WA = OFF_GD + GATE_LORA
RW_COLS = OFF_WA + 256
OFF_CQ = RW_COLS
OFF_CKV = OFF_CQ + Q_RANK
OFF_KR = OFF_CKV + KV_RANK
NP_COLS = 4608
MLA_BLK = 512


def _perm_cols(a):
    z = lambda n: jnp.zeros(a.shape[:-1] + (n,), a.dtype)
    o_wd = 3 * RWKV_WIDTH
    o_gd = o_wd + DECAY_LORA + AAA_LORA
    pieces = [a[..., :o_wd], a[..., o_gd:N_SHIFT], a[..., o_wd:o_gd], z(256 - DECAY_LORA - AAA_LORA)]
    if a.shape[-1] == N_IN:
        pieces += [a[..., N_SHIFT:N_IN], z(NP_COLS - OFF_KR - MLA_ROPE)]
    return jnp.concatenate(pieces, -1)


def _unperm_shift_cols(a):
    return jnp.concatenate([a[..., :OFF_GD], a[..., OFF_WA:OFF_WA + DECAY_LORA + AAA_LORA],
                            a[..., OFF_GD:OFF_WA]], -1)


def _params(**kw):
    return pltpu.CompilerParams(vmem_limit_bytes=VMEM_LIMIT, **kw)


def _sigmoid(x):
    return 1.0 / (1.0 + jnp.exp(-x))


def _layernorm(x, g, b):
    mu = jnp.mean(x, -1, keepdims=True)
    xc = x - mu
    var = jnp.mean(xc * xc, -1, keepdims=True)
    return xc * lax.rsqrt(var + LN_EPS) * g + b


def _rmsnorm(x, g):
    return x * lax.rsqrt(jnp.mean(x * x, -1, keepdims=True) + RMS_EPS) * g


def _ada_body(c_ref, w_ref, b_ref, o_ref):
    c = c_ref[...]
    a = (c * _sigmoid(c)).astype(BF16)
    o_ref[...] = jnp.dot(a, w_ref[...].astype(BF16), preferred_element_type=F32) + b_ref[...]


def _ada(c, w_ada, b_ada):
    rows, d = c.shape
    n = w_ada.shape[1]
    tn = 1024
    return pl.pallas_call(
        _ada_body,
        grid=(n // tn,),
        in_specs=[pl.BlockSpec((rows, d), lambda j: (0, 0)),
                  pl.BlockSpec((d, tn), lambda j: (0, j)),
                  pl.BlockSpec((1, tn), lambda j: (0, j))],
        out_specs=pl.BlockSpec((rows, tn), lambda j: (0, j)),
        out_shape=jax.ShapeDtypeStruct((rows, n), F32),
        compiler_params=_params(dimension_semantics=("parallel",)),
        name="ada_mod",
    )(c, w_ada, b_ada.reshape(1, n))


def _grouped_spec(tm, d, col=0):
    return pl.BlockSpec((None, tm, d), lambda b, r, *_: (b, r, col))


def _time_major_spec(tm, d):
    return pl.BlockSpec((tm, d), lambda b, r, *_: (r, b))


def _mod_spec(mod, tm, chunk):
    if mod.shape[1] == 1:
        return pl.BlockSpec((None, 1, D_MODEL), lambda b, r, *_: (b, 0, chunk))
    return pl.BlockSpec((None, tm, D_MODEL), lambda b, r, *_: (b, r, chunk))


def _const_spec(shape):
    nd = len(shape)
    return pl.BlockSpec(shape, lambda *_: (0,) * nd)


def _inproj_body(x_ref, sc_ref, sh_ref, w_ref, o_ref, h_scr):
    @pl.when(pl.program_id(2) == 0)
    def _():
        h = x_ref[...] * (1.0 + sc_ref[...]) + sh_ref[...]
        h_scr[...] = h.astype(BF16)

    o_ref[...] = jnp.dot(h_scr[...], w_ref[...], preferred_element_type=F32)


def _in_proj(x, mod, w_in_p):
    B, T, D = x.shape
    tm = min(512, T)
    nj = 2
    tn = NP_COLS // nj
    return pl.pallas_call(
        _inproj_body,
        grid=(B, T // tm, nj),
        in_specs=[_grouped_spec(tm, D), _mod_spec(mod, tm, 1), _mod_spec(mod, tm, 0),
                  pl.BlockSpec((D, tn), lambda b, r, j: (0, j))],
        out_specs=pl.BlockSpec((None, tm, tn), lambda b, r, j: (b, r, j)),
        out_shape=jax.ShapeDtypeStruct((B, T, NP_COLS), F32),
        scratch_shapes=[pltpu.VMEM((tm, D), BF16)],
        compiler_params=_params(dimension_semantics=("parallel", "parallel", "arbitrary")),
        name="in_proj",
    )(x, mod, mod, w_in_p)


SUBLANES = 8


def _rwkv_pre_body(p_ref, pb_ref, prev_ref, mu_ref, wa0_ref, wl_ref, g2_ref,
                   r_ref, k_ref, v_ref, w_ref, a_ref, g_ref, *, shift):
    p = p_ref[...]
    at_start = pl.program_id(1) == 0
    if shift == 1:
        first = jnp.where(at_start, prev_ref[...], pb_ref[SUBLANES - 1:SUBLANES, :])
        row = lax.broadcasted_iota(jnp.int32, p.shape, 0)
        shifted = jnp.where(row == 0, first, pltpu.roll(p, 1, axis=0))
    else:
        first = jnp.where(at_start, prev_ref[...], pb_ref[...])
        shifted = jnp.concatenate([first, p[:-shift]], axis=0)
    z = p + mu_ref[...] * (shifted - p)
    r_ref[...] = z[:, OFF_R:OFF_R + RWKV_WIDTH]
    k_ref[...] = z[:, OFF_K:OFF_K + RWKV_WIDTH]
    v_ref[...] = z[:, OFF_V:OFF_V + RWKV_WIDTH]
    wa = z[:, OFF_WA:OFF_WA + 256]
    lane = lax.broadcasted_iota(jnp.int32, wa.shape, 1)
    lora_in = jnp.where(lane < DECAY_LORA, jnp.tanh(wa), wa).astype(BF16)
    pre = jnp.dot(lora_in, wl_ref[...], preferred_element_type=F32) + wa0_ref[...]
    u = pre[:, :RWKV_WIDTH]
    w_log = jnp.minimum(u, 0.0) - jnp.log1p(jnp.exp(-jnp.abs(u))) - 0.5
    w_ref[...] = jnp.exp(-jnp.exp(w_log))
    a_ref[...] = _sigmoid(pre[:, RWKV_WIDTH:])
    gd = _sigmoid(z[:, OFF_GD:OFF_GD + GATE_LORA]).astype(BF16)
    g_ref[...] = jnp.dot(gd, g2_ref[...], preferred_element_type=F32)


def _rwkv_pre(proj, prev_p, shift, mu_p, wa0, w_lora, g2):
    B, T, _ = proj.shape
    tm = min(256, T)
    back = max(shift, SUBLANES)
    assert tm % back == 0 and prev_p.shape[1] == shift
    blk = _time_major_spec(tm, RWKV_WIDTH)
    out = jax.ShapeDtypeStruct((T, B * RWKV_WIDTH), F32)
    return pl.pallas_call(
        functools.partial(_rwkv_pre_body, shift=shift),
        grid=(B, T // tm),
        in_specs=[_grouped_spec(tm, RW_COLS),
                  pl.BlockSpec((None, back, RW_COLS), lambda b, r: (b, jnp.maximum(r * (tm // back) - 1, 0), 0)),
                  pl.BlockSpec((None, shift, RW_COLS), lambda b, r: (b, 0, 0)),
                  _const_spec((1, RW_COLS)), _const_spec((1, 2 * RWKV_WIDTH)),
                  _const_spec((256, 2 * RWKV_WIDTH)), _const_spec((GATE_LORA, RWKV_WIDTH))],
        out_specs=[blk] * 5 + [_grouped_spec(tm, RWKV_WIDTH)],
        out_shape=[out] * 5 + [jax.ShapeDtypeStruct((B, T, RWKV_WIDTH), F32)],
        compiler_params=_params(dimension_semantics=("parallel", "parallel")),
        name="rwkv_pre",
    )(proj, proj, prev_p, mu_p, wa0, w_lora, g2)


WKV_UNROLL = 8


def _wkv_body(r_ref, k_ref, v_ref, w_ref, a_ref, kk_p, ka_p, rk_p, lg_p, lb_p, s0_ref,
              o_ref, sout_ref, S, b_kk, b_wr, b_kka, b_k2, b_rows):
    tt = r_ref.shape[0]
    N = RWKV_HEAD

    @pl.when(pl.program_id(1) == 0)
    def _():
        S[...] = s0_ref[...]

    r = r_ref[...]
    k = k_ref[...]
    a = a_ref[...]
    kk = k * kk_p[...]
    nrm = jnp.sqrt(jnp.sum(kk * kk, axis=1, keepdims=True))
    kk = kk / jnp.maximum(nrm, 1e-12)
    k2 = k * (1.0 + (a - 1.0) * ka_p[...])
    kka = kk * a
    b_kk[...] = kk
    b_wr[...] = w_ref[...] * r
    b_kka[...] = kka
    b_k2[...] = k2
    b_rows[:, 0:1, :] = jnp.sum(kka * r, axis=1, keepdims=True)
    b_rows[:, 1:2, :] = jnp.sum(k2 * r, axis=1, keepdims=True)
    b_rows[:, 2:3, :] = jnp.sum(r * k2 * rk_p[...], axis=1, keepdims=True)

    def tok(t, carry):
        def reduce_pass(j0, acc):
            sa, o1 = acc
            for jj in range(WKV_UNROLL):
                j = j0 * WKV_UNROLL + jj
                Sj = S[j]
                sa = sa + Sj * b_kk[t, pl.ds(j, 1), :]
                o1 = o1 + Sj * b_wr[t, pl.ds(j, 1), :]
            return sa, o1

        zero = jnp.zeros((N, LANES), F32)
        sa, o1 = lax.fori_loop(0, N // WKV_UNROLL, reduce_pass, (zero, zero))
        vT = v_ref[t]
        o_ref[t] = o1 - sa * b_rows[t, 0:1, :] + vT * b_rows[t, 1:2, :]

        def update_pass(j0, c):
            for jj in range(WKV_UNROLL):
                j = j0 * WKV_UNROLL + jj
                S[j] = (S[j] * w_ref[t, pl.ds(j, 1), :] - sa * b_kka[t, pl.ds(j, 1), :]
                        + vT * b_k2[t, pl.ds(j, 1), :])
            return c

        lax.fori_loop(0, N // WKV_UNROLL, update_pass, 0)
        return carry

    lax.fori_loop(0, tt, tok, 0)

    o = o_ref[...]
    mu = jnp.mean(o, axis=1, keepdims=True)
    oc = o - mu
    var = jnp.mean(oc * oc, axis=1, keepdims=True)
    on = oc * lax.rsqrt(var + GN_EPS) * lg_p[...] + lb_p[...]
    o_ref[...] = on + b_rows[:, 2:3, :] * v_ref[...]

    @pl.when(pl.program_id(1) == pl.num_programs(1) - 1)
    def _():
        sout_ref[...] = S[...]


def _wkv(r, k, v, w, a, tiles, s0):
    T, G, N, _ = r.shape
    tt = min(32, T)
    tok_spec = pl.BlockSpec((tt, None, N, LANES), lambda g_, i: (i, g_, 0, 0))
    st_spec = pl.BlockSpec((None, N, N, LANES), lambda g_, i: (g_, 0, 0, 0))
    per_group = tiles[0].shape[0] > 1
    tile_spec = pl.BlockSpec((None, N, LANES), lambda g_, i: (g_ if per_group else 0, 0, 0))
    return pl.pallas_call(
        _wkv_body,
        grid=(G, T // tt),
        in_specs=[tok_spec] * 5 + [tile_spec] * 5 + [st_spec],
        out_specs=[tok_spec, st_spec],
        out_shape=[jax.ShapeDtypeStruct((T, G, N, LANES), F32), jax.ShapeDtypeStruct((G, N, N, LANES), F32)],
        scratch_shapes=[pltpu.VMEM((N, N, LANES), F32)] + [pltpu.VMEM((tt, N, LANES), F32)] * 4
                       + [pltpu.VMEM((tt, SUBLANES, LANES), F32)],
        compiler_params=_params(dimension_semantics=("parallel", "arbitrary")),
        name="wkv_scan",
    )(r, k, v, w, a, *tiles, s0)


def _rot_half(x):
    n = x.shape[-1]
    lane = lax.broadcasted_iota(jnp.int32, x.shape, x.ndim - 1)
    fwd = pltpu.roll(x, n - MLA_ROPE // 2, axis=x.ndim - 1)
    bwd = pltpu.roll(x, MLA_ROPE // 2, axis=x.ndim - 1)
    return jnp.where(lane % MLA_ROPE < MLA_ROPE // 2, fwd, bwd)


def _mla_body(cq_ref, ckr_ref, cos_ref, sin_ref, qn_ref, wq_ref, wuk_ref, kvn_ref,
              q_ref, kc_ref, lat_ref, kr_ref, latt_ref):
    cq = cq_ref[...]
    q = jnp.dot(_rmsnorm(cq, qn_ref[...]).astype(BF16), wq_ref[...], preferred_element_type=F32)
    nope_w = MLA_HEADS * MLA_NOPE
    qr = q[:, nope_w:]
    q_rope = qr * cos_ref[...] + _rot_half(qr) * sin_ref[...]
    for h in range(MLA_HEADS):
        qn = q[:, h * MLA_NOPE:(h + 1) * MLA_NOPE].astype(BF16)
        q_ref[h, :, :KV_RANK] = jnp.dot(qn, wuk_ref[h], preferred_element_type=F32).astype(BF16)
        q_ref[h, :, KV_RANK:] = q_rope[:, h * MLA_ROPE:(h + 1) * MLA_ROPE].astype(BF16)
    ckr = ckr_ref[...]
    lat = _rmsnorm(ckr[:, :KV_RANK], kvn_ref[...])
    slab = ckr[:, KV_RANK:KV_RANK + LANES]
    kr = (slab * cos_ref[:, :LANES] + _rot_half(slab) * sin_ref[:, :LANES])[:, :MLA_ROPE]
    lat_ref[...] = lat
    kr_ref[...] = kr
    kc_ref[:, :KV_RANK] = lat.astype(BF16)
    kc_ref[:, KV_RANK:] = kr.astype(BF16)
    latt_ref[...] = lat.T.astype(BF16)


def _mla_proj(proj, cos_t, sin_t, q_norm, wq_p, wuk_t, kv_norm):
    B, T, _ = proj.shape
    tm = min(512, T)
    qk = KV_RANK + MLA_ROPE
    tab = pl.BlockSpec((None, tm, MLA_HEADS * MLA_ROPE), lambda b, r: (0, r, 0))
    return pl.pallas_call(
        _mla_body,
        grid=(B, T // tm),
        in_specs=[_grouped_spec(tm, MLA_BLK, OFF_CQ // MLA_BLK), _grouped_spec(tm, MLA_BLK, OFF_CKV // MLA_BLK),
                  tab, tab, _const_spec((1, Q_RANK)), _const_spec(wq_p.shape), _const_spec(wuk_t.shape),
                  _const_spec((1, KV_RANK))],
        out_specs=[pl.BlockSpec((None, MLA_HEADS, tm, qk), lambda b, r: (b, 0, r, 0)),
                   _grouped_spec(tm, qk), _grouped_spec(tm, KV_RANK), _grouped_spec(tm, MLA_ROPE),
                   pl.BlockSpec((None, KV_RANK, tm), lambda b, r: (b, 0, r))],
        out_shape=[jax.ShapeDtypeStruct((B, MLA_HEADS, T, qk), BF16),
                   jax.ShapeDtypeStruct((B, T, qk), BF16),
                   jax.ShapeDtypeStruct((B, T, KV_RANK), F32),
                   jax.ShapeDtypeStruct((B, T, MLA_ROPE), F32),
                   jax.ShapeDtypeStruct((B, KV_RANK, T), BF16)],
        compiler_params=_params(dimension_semantics=("parallel", "parallel")),
        name="mla_proj",
    )(proj, proj, cos_t, sin_t, q_norm, wq_p, wuk_t, kv_norm)


HEADS_PER_DOT = 4


def _pattn_body(q_ref, k_ref, vt_ref, o_ref, m_scr, l_scr, acc_scr, *, tq, tk):
    i = pl.program_id(1)
    j = pl.program_id(2)
    last_j = ((i + 1) * tq - 1) // tk

    @pl.when(j == 0)
    def _():
        m_scr[...] = jnp.full(m_scr.shape, NEG_BIG, F32)
        l_scr[...] = jnp.zeros(l_scr.shape, F32)
        acc_scr[...] = jnp.zeros(acc_scr.shape, F32)

    def step(masked):
        k = k_ref[...]
        vt = vt_ref[...]
        if masked:
            cols = HEADS_PER_DOT * tq
            kpos = j * tk + lax.broadcasted_iota(jnp.int32, (tk, cols), 0)
            qpos = i * tq + lax.broadcasted_iota(jnp.int32, (tk, cols), 1) % tq
            keep = kpos <= qpos
        for g in range(MLA_HEADS // HEADS_PER_DOT):
            q = q_ref[g * HEADS_PER_DOT:(g + 1) * HEADS_PER_DOT].reshape(HEADS_PER_DOT * tq, q_ref.shape[-1])
            st = lax.dot_general(k, q, (((1,), (1,)), ((), ())),
                                 preferred_element_type=F32) * ATTN_SCALE
            if masked:
                st = jnp.where(keep, st, NEG_BIG)
            m_prev = m_scr[g]
            m_new = jnp.maximum(m_prev, jnp.max(st, axis=0, keepdims=True))
            alpha = jnp.exp(m_prev - m_new)
            pt = jnp.exp(st - m_new)
            l_scr[g] = alpha * l_scr[g] + jnp.sum(pt, axis=0, keepdims=True)
            acc_scr[g] = alpha * acc_scr[g] + jnp.dot(vt, pt.astype(BF16), preferred_element_type=F32)
            m_scr[g] = m_new

    crosses_diagonal = (j + 1) * tk - 1 > i * tq
    pl.when(jnp.logical_and(j <= last_j, crosses_diagonal))(functools.partial(step, True))
    pl.when(jnp.logical_and(j <= last_j, jnp.logical_not(crosses_diagonal)))(functools.partial(step, False))

    @pl.when(j == pl.num_programs(2) - 1)
    def _():
        for g in range(MLA_HEADS // HEADS_PER_DOT):
            ctx_t = acc_scr[g] / l_scr[g]
            for hh in range(HEADS_PER_DOT):
                o_ref[g * HEADS_PER_DOT + hh] = ctx_t[:, hh * tq:(hh + 1) * tq].T.astype(BF16)


def _prompt_attn(q4, kc, lat_t):
    B, H, T, qk = q4.shape
    tq = min(256, T)
    tk = min(512, T)
    body = functools.partial(_pattn_body, tq=tq, tk=tk)
    kv_block = lambda i, j: jnp.minimum(j, ((i + 1) * tq - 1) // tk)
    return pl.pallas_call(
        body,
        grid=(B, T // tq, T // tk),
        in_specs=[pl.BlockSpec((None, H, tq, qk), lambda b, i, j: (b, 0, i, 0)),
                  pl.BlockSpec((None, tk, qk), lambda b, i, j: (b, kv_block(i, j), 0)),
                  pl.BlockSpec((None, KV_RANK, tk), lambda b, i, j: (b, 0, kv_block(i, j)))],
        out_specs=pl.BlockSpec((None, H, tq, KV_RANK), lambda b, i, j: (b, 0, i, 0)),
        out_shape=jax.ShapeDtypeStruct((B, H, T, KV_RANK), BF16),
        scratch_shapes=[pltpu.VMEM((H // HEADS_PER_DOT, 1, HEADS_PER_DOT * tq), F32),
                        pltpu.VMEM((H // HEADS_PER_DOT, 1, HEADS_PER_DOT * tq), F32),
                        pltpu.VMEM((H // HEADS_PER_DOT, KV_RANK, HEADS_PER_DOT * tq), F32)],
        compiler_params=_params(dimension_semantics=("parallel", "parallel", "arbitrary")),
        name="prompt_attn",
    )(q4, kc, lat_t)


SAMPLE_PAGES_PER_CHUNK = 64


def _sattn_body(pt_ref, q_ref, latn_ref, krn_ref, lat_hbm, kr_hbm, o_ref,
                lat_buf, kr_buf, sem, m_scr, l_scr, acc_scr, *, n_pg, n_chunks, n_new):
    seq = pl.program_id(0)
    nt = (((1,), (1,)), ((), ()))

    def page_copies(page_of, slot):
        for p in range(n_pg):
            page = page_of(p)
            yield pltpu.make_async_copy(lat_hbm.at[page], lat_buf.at[slot, p], sem.at[slot, 0])
            yield pltpu.make_async_copy(kr_hbm.at[page], kr_buf.at[slot, p], sem.at[slot, 1])

    def start_chunk(s, chunk, slot):
        for cp in page_copies(lambda p: pt_ref[s, chunk * n_pg + p], slot):
            cp.start()

    def wait_chunk(slot):
        for cp in page_copies(lambda p: 0, slot):
            cp.wait()

    @pl.when(seq == 0)
    def _():
        start_chunk(seq, 0, 0)

    m_scr[...] = jnp.full(m_scr.shape, NEG_BIG, F32)
    l_scr[...] = jnp.zeros(l_scr.shape, F32)
    acc_scr[...] = jnp.zeros(acc_scr.shape, F32)

    q = q_ref[...]
    q_lat = q[:, :KV_RANK]
    q_rope = q[:, KV_RANK:]

    def update(s, vals):
        m_prev = m_scr[...]
        m_new = jnp.maximum(m_prev, jnp.max(s, axis=1, keepdims=True))
        alpha = jnp.exp(m_prev - m_new)
        p = jnp.exp(s - m_new)
        l_scr[...] = alpha * l_scr[...] + jnp.sum(p, axis=1, keepdims=True)
        acc_scr[...] = alpha * acc_scr[...] + jnp.dot(p.astype(BF16), vals, preferred_element_type=F32)
        m_scr[...] = m_new

    for chunk in range(n_chunks):
        slot = chunk % 2
        if chunk + 1 < n_chunks:
            start_chunk(seq, chunk + 1, 1 - slot)
        else:
            @pl.when(seq + 1 < pl.num_programs(0))
            def _():
                start_chunk(seq + 1, 0, 1 - slot)
        wait_chunk(slot)
        lat_all = lat_buf[slot].reshape(n_pg * lat_buf.shape[2], KV_RANK).astype(BF16)
        krt_all = jnp.concatenate([kr_buf[slot, p].astype(BF16) for p in range(n_pg)], axis=1)
        s = lax.dot_general(q_lat, lat_all, nt, preferred_element_type=F32)
        s = (s + jnp.dot(q_rope, krt_all, preferred_element_type=F32)) * ATTN_SCALE
        update(s, lat_all)

    latn = latn_ref[...].astype(BF16)
    sn = lax.dot_general(q_lat, latn, nt, preferred_element_type=F32)
    sn = (sn + lax.dot_general(q_rope, krn_ref[...].astype(BF16), nt, preferred_element_type=F32)) * ATTN_SCALE
    row_t = lax.broadcasted_iota(jnp.int32, sn.shape, 0) % n_new
    col_t = lax.broadcasted_iota(jnp.int32, sn.shape, 1)
    sn = jnp.where(col_t <= row_t, sn, NEG_BIG)
    update(sn, latn)
    o_ref[...] = acc_scr[...] / l_scr[...]


def _sample_attn(q, latn, krn, cache_lat, cache_kr, page_table):
    S, rows, qk = q.shape
    n_pages = page_table.shape[1]
    page = cache_lat.shape[1]
    n_pg = min(SAMPLE_PAGES_PER_CHUNK, n_pages // 2)
    n_chunks = n_pages // n_pg
    assert n_pages % n_pg == 0 and n_chunks % 2 == 0
    n_new = rows // MLA_HEADS
    pad_new = latn.shape[1]
    grid_spec = pltpu.PrefetchScalarGridSpec(
        num_scalar_prefetch=1,
        grid=(S,),
        in_specs=[pl.BlockSpec((None, rows, qk), lambda s, pt: (s, 0, 0)),
                  pl.BlockSpec((None, pad_new, KV_RANK), lambda s, pt: (s, 0, 0)),
                  pl.BlockSpec((None, pad_new, MLA_ROPE), lambda s, pt: (s, 0, 0)),
                  pl.BlockSpec(memory_space=pl.ANY), pl.BlockSpec(memory_space=pl.ANY)],
        out_specs=pl.BlockSpec((None, rows, KV_RANK), lambda s, pt: (s, 0, 0)),
        scratch_shapes=[pltpu.VMEM((2, n_pg, page, KV_RANK), F32), pltpu.VMEM((2, n_pg, MLA_ROPE, page), F32),
                        pltpu.SemaphoreType.DMA((2, 2)),
                        pltpu.VMEM((rows, 1), F32), pltpu.VMEM((rows, 1), F32),
                        pltpu.VMEM((rows, KV_RANK), F32)],
    )
    return pl.pallas_call(
        functools.partial(_sattn_body, n_pg=n_pg, n_chunks=n_chunks, n_new=n_new),
        grid_spec=grid_spec,
        out_shape=jax.ShapeDtypeStruct((S, rows, KV_RANK), F32),
        compiler_params=_params(dimension_semantics=("arbitrary",)),
        name="sample_attn",
    )(page_table, q, latn, krn, cache_lat, cache_kr)


def _uv_body(ctx_ref, wuv_ref, on_ref, o_ref):
    parts = [jnp.dot(ctx_ref[h].astype(BF16), wuv_ref[h], preferred_element_type=F32)
             for h in range(MLA_HEADS)]
    om = jnp.concatenate(parts, axis=1)
    o_ref[...] = _rmsnorm(om, on_ref[...]).astype(BF16)


def _uv_norm(ctx, wuv_t, out_norm):
    B, H, T, R = ctx.shape
    tm = min(512, T)
    return pl.pallas_call(
        _uv_body,
        grid=(B, T // tm),
        in_specs=[pl.BlockSpec((None, H, tm, R), lambda b, r: (b, 0, r, 0)),
                  _const_spec(wuv_t.shape), _const_spec((1, MLA_WIDTH))],
        out_specs=_grouped_spec(tm, MLA_WIDTH),
        out_shape=jax.ShapeDtypeStruct((B, T, MLA_WIDTH), BF16),
        compiler_params=_params(dimension_semantics=("parallel", "parallel")),
        name="uv_norm",
    )(ctx, wuv_t, out_norm)


def _outproj_body(or_ref, gate_ref, om_ref, x_ref, g1_ref, w_ref, lg_ref, lb_ref, o_ref):
    o_r = (or_ref[...] * gate_ref[...]).astype(BF16)
    mixed = jnp.dot(o_r, w_ref[:RWKV_WIDTH, :], preferred_element_type=F32)
    mixed = mixed + jnp.dot(om_ref[...], w_ref[RWKV_WIDTH:, :], preferred_element_type=F32)
    y = DEEPNORM_ALPHA * x_ref[...] + g1_ref[...] * mixed
    o_ref[...] = _layernorm(y, lg_ref[...], lb_ref[...])


def _out_proj(o_r, gate, o_m, x, mod, w_out, ln_g, ln_b):
    B, T, D = x.shape
    tm = min(256, T)
    return pl.pallas_call(
        _outproj_body,
        grid=(B, T // tm),
        in_specs=[_time_major_spec(tm, RWKV_WIDTH), _grouped_spec(tm, RWKV_WIDTH),
                  _grouped_spec(tm, MLA_WIDTH), _grouped_spec(tm, D),
                  _mod_spec(mod, tm, 2), _const_spec((D, D)), _const_spec((1, D)), _const_spec((1, D))],
        out_specs=_grouped_spec(tm, D),
        out_shape=jax.ShapeDtypeStruct((B, T, D), F32),
        compiler_params=_params(dimension_semantics=("parallel", "parallel")),
        name="out_proj_ln1",
    )(o_r, gate, o_m, x, mod, w_out, ln_g, ln_b)


def _ffn_body(x_ref, sc_ref, sh_ref, g2_ref, wu_ref, wd_ref, lg_ref, lb_ref, o_ref, h_scr, acc_scr):
    f = pl.program_id(2)

    @pl.when(f == 0)
    def _():
        h_scr[...] = (x_ref[...] * (1.0 + sc_ref[...]) + sh_ref[...]).astype(BF16)
        acc_scr[...] = jnp.zeros(acc_scr.shape, F32)

    u = jnp.maximum(jnp.dot(h_scr[...], wu_ref[...], preferred_element_type=F32), 0.0)
    acc_scr[...] += jnp.dot((u * u).astype(BF16), wd_ref[...], preferred_element_type=F32)

    @pl.when(f == pl.num_programs(2) - 1)
    def _():
        y = DEEPNORM_ALPHA * x_ref[...] + g2_ref[...] * acc_scr[...]
        o_ref[...] = _layernorm(y, lg_ref[...], lb_ref[...])


def _ffn(x, mod, w_up, w_down, ln_g, ln_b):
    B, T, D = x.shape
    tm = min(512, T)
    tf = 1024
    return pl.pallas_call(
        _ffn_body,
        grid=(B, T // tm, D_FF // tf),
        in_specs=[_grouped_spec(tm, D), _mod_spec(mod, tm, 4), _mod_spec(mod, tm, 3), _mod_spec(mod, tm, 5),
                  pl.BlockSpec((D, tf), lambda b, r, f: (0, f)),
                  pl.BlockSpec((tf, D), lambda b, r, f: (f, 0)),
                  _const_spec((1, D)), _const_spec((1, D))],
        out_specs=_grouped_spec(tm, D),
        out_shape=jax.ShapeDtypeStruct((B, T, D), F32),
        scratch_shapes=[pltpu.VMEM((tm, D), BF16), pltpu.VMEM((tm, D), F32)],
        compiler_params=_params(dimension_semantics=("parallel", "parallel", "arbitrary")),
        name="ffn_ln2",
    )(x, mod, mod, mod, w_up, w_down, ln_g, ln_b)


def _rope_tables(pos):
    half = MLA_ROPE // 2
    inv = ROPE_THETA ** (-jnp.arange(half, dtype=F32) / half)
    ang = pos.astype(F32)[:, None] * inv
    cos, sin = jnp.cos(ang), jnp.sin(ang)
    cos_t = jnp.tile(jnp.concatenate([cos, cos], -1), (1, MLA_HEADS))
    sin_t = jnp.tile(jnp.concatenate([-sin, sin], -1), (1, MLA_HEADS))
    return cos_t[None], sin_t[None]


SEQ_PER_GROUP = LANES // RWKV_HEADS


class _PromptLanes:
    @staticmethod
    def to_lanes(x, n_seq):
        T = x.shape[0]
        return x.reshape(T, n_seq // SEQ_PER_GROUP, LANES, RWKV_HEAD).transpose(0, 1, 3, 2)

    @staticmethod
    def from_lanes(o):
        T, G = o.shape[:2]
        return o.transpose(0, 1, 3, 2).reshape(T, G * LANES * RWKV_HEAD)

    @staticmethod
    def tile(p, n_seq):
        return jnp.tile(p.reshape(RWKV_HEADS, RWKV_HEAD).T, (1, SEQ_PER_GROUP))[None]

    @staticmethod
    def state_to_lanes(s):
        B = s.shape[0]
        s = s.reshape(B // SEQ_PER_GROUP, LANES, RWKV_HEAD, RWKV_HEAD)
        return s.transpose(0, 3, 2, 1)

    @staticmethod
    def state_from_lanes(s):
        G = s.shape[0]
        return s.transpose(0, 3, 2, 1).reshape(G * SEQ_PER_GROUP, RWKV_HEADS, RWKV_HEAD, RWKV_HEAD)


class _SampleLanes:
    @staticmethod
    def to_lanes(x, n_seq):
        x = x.reshape(-1, n_seq // LANES, LANES, RWKV_HEADS, RWKV_HEAD)
        return x.transpose(0, 1, 3, 4, 2).reshape(x.shape[0], (n_seq // LANES) * RWKV_HEADS, RWKV_HEAD, LANES)

    @staticmethod
    def from_lanes(o):
        Td, G = o.shape[:2]
        o = o.reshape(Td, G // RWKV_HEADS, RWKV_HEADS, RWKV_HEAD, LANES)
        return o.transpose(0, 1, 4, 2, 3).reshape(Td * (G // RWKV_HEADS) * LANES, RWKV_WIDTH)

    @staticmethod
    def tile(p, n_seq):
        t = jnp.broadcast_to(p.reshape(RWKV_HEADS, RWKV_HEAD, 1), (RWKV_HEADS, RWKV_HEAD, LANES))
        return jnp.tile(t, (n_seq // LANES, 1, 1))

    @staticmethod
    def state_to_lanes(s):
        S = s.shape[0]
        s = s.reshape(S // LANES, LANES, RWKV_HEADS, RWKV_HEAD, RWKV_HEAD)
        return s.transpose(0, 2, 4, 3, 1).reshape((S // LANES) * RWKV_HEADS, RWKV_HEAD, RWKV_HEAD, LANES)

    @staticmethod
    def state_from_lanes(s):
        G = s.shape[0]
        s = s.reshape(G // RWKV_HEADS, RWKV_HEADS, RWKV_HEAD, RWKV_HEAD, LANES)
        return s.transpose(0, 4, 1, 3, 2).reshape((G // RWKV_HEADS) * LANES, RWKV_HEADS, RWKV_HEAD, RWKV_HEAD)


def _rwkv_mix(proj, prev_p, state, lanes, wp):
    n_seq = state.shape[0]
    r, k, v, w, a, g = _rwkv_pre(proj, prev_p, prev_p.shape[1], wp['mu'], wp['wa0'], wp['w_lora'], wp['g2'])
    tiles = [lanes.tile(p, n_seq) for p in wp['head_params']]
    o, s_out = _wkv(*(lanes.to_lanes(t, n_seq) for t in (r, k, v, w, a)), tiles, lanes.state_to_lanes(state))
    return lanes.from_lanes(o), g, lanes.state_from_lanes(s_out)


def _layer_back(x, mod, o_r, gate, ctx, wp):
    o_m = _uv_norm(ctx, wp['wuv'], wp['out_norm'])
    x1 = _out_proj(o_r, gate, o_m, x, mod, wp['w_out'], wp['ln1_g'], wp['ln1_b'])
    return _ffn(x1, mod, wp['w_up'], wp['w_down'], wp['ln2_g'], wp['ln2_b'])


def kernel(x_prompt, x_sample, c_prompt, c_sample, cache_latent, cache_krope, state_wkv, state_shift, page_table, w_ada, b_ada, w_in, rwkv_mu, rwkv_w0, rwkv_w2, rwkv_a0, rwkv_a2, rwkv_g2, rwkv_k_k, rwkv_k_a, rwkv_r_k, rwkv_lnx_g, rwkv_lnx_b, mla_q_norm, mla_w_q_up, mla_kv_norm, mla_w_uk, mla_w_uv, mla_out_norm, w_out, ln1_g, ln1_b, w_up, w_down, ln2_g, ln2_b):
    B, T, D = x_prompt.shape
    S, Td, _ = x_sample.shape
    past = page_table.shape[1] * cache_latent.shape[2]
    l = 0

    w_lora = jnp.zeros((256, 2 * RWKV_WIDTH), F32)
    w_lora = w_lora.at[:DECAY_LORA, :RWKV_WIDTH].set(rwkv_w2[l])
    w_lora = w_lora.at[DECAY_LORA:DECAY_LORA + AAA_LORA, RWKV_WIDTH:].set(rwkv_a2[l])
    wq = mla_w_q_up[l].reshape(Q_RANK, MLA_HEADS, MLA_NOPE + MLA_ROPE)
    wq_p = jnp.concatenate([wq[:, :, :MLA_NOPE].reshape(Q_RANK, -1), wq[:, :, MLA_NOPE:].reshape(Q_RANK, -1)], -1)
    row = lambda p: p.reshape(1, -1)
    wp = dict(
        w_in=_perm_cols(w_in[l]).astype(BF16),
        mu=row(_perm_cols(rwkv_mu[l])),
        wa0=row(jnp.concatenate([rwkv_w0[l], rwkv_a0[l]])),
        w_lora=w_lora.astype(BF16),
        g2=rwkv_g2[l].astype(BF16),
        head_params=[rwkv_k_k[l], rwkv_k_a[l], rwkv_r_k[l].reshape(-1), rwkv_lnx_g[l], rwkv_lnx_b[l]],
        q_norm=row(mla_q_norm[l]), wq=wq_p.astype(BF16),
        wuk=mla_w_uk[l].transpose(1, 2, 0).astype(BF16),
        kv_norm=row(mla_kv_norm[l]),
        wuv=mla_w_uv[l].transpose(1, 0, 2).astype(BF16),
        out_norm=row(mla_out_norm[l]),
        w_out=w_out[l].astype(BF16), ln1_g=row(ln1_g[l]), ln1_b=row(ln1_b[l]),
        w_up=w_up[l].astype(BF16), w_down=w_down[l].astype(BF16), ln2_g=row(ln2_g[l]), ln2_b=row(ln2_b[l]),
    )

    mod = _ada(jnp.concatenate([c_prompt, c_sample], 0), w_ada[l], b_ada[l])
    mod_p = mod[:B, None, :]
    mod_s = jnp.tile(mod[B:], (Td, 1))[None]

    s0_p = jnp.zeros((B, RWKV_HEADS, RWKV_HEAD, RWKV_HEAD), F32)
    prev0 = jnp.zeros((B, 1, RW_COLS), F32)
    proj_p = _in_proj(x_prompt, mod_p, wp['w_in'])
    o_r, gate_p, wkv_p = _rwkv_mix(proj_p, prev0, s0_p, _PromptLanes, wp)
    cos_t, sin_t = _rope_tables(jnp.arange(T))
    q4, kc, lat_p, kr_p, lat_t = _mla_proj(proj_p, cos_t, sin_t, wp['q_norm'], wp['wq'], wp['wuk'],
                                           wp['kv_norm'])
    last_p = _unperm_shift_cols(proj_p[:, T - 1, :RW_COLS])
    ctx_p = _prompt_attn(q4, kc, lat_t)
    y_p = _layer_back(x_prompt, mod_p, o_r, gate_p, ctx_p, wp)

    xs = x_sample.transpose(1, 0, 2).reshape(1, Td * S, D)
    pos_s = jnp.repeat(past + jnp.arange(Td), S)
    prev_s = _perm_cols(state_shift[l])[None]
    proj = _in_proj(xs, mod_s, wp['w_in'])
    o_rs, gate_s, wkv_s = _rwkv_mix(proj, prev_s, state_wkv[l], _SampleLanes, wp)
    cos_t, sin_t = _rope_tables(pos_s)
    q4s, _, lat_s, kr_s, _ = _mla_proj(proj, cos_t, sin_t, wp['q_norm'], wp['wq'], wp['wuk'], wp['kv_norm'])
    last_s = _unperm_shift_cols(proj[0, (Td - 1) * S:, :RW_COLS])
    qk = KV_RANK + MLA_ROPE
    q_s = q4s.reshape(MLA_HEADS, Td, S, qk).transpose(2, 0, 1, 3).reshape(S, MLA_HEADS * Td, qk)
    lat_s = lat_s.reshape(Td, S, KV_RANK).transpose(1, 0, 2)
    kr_s = kr_s.reshape(Td, S, MLA_ROPE).transpose(1, 0, 2)
    pad16 = lambda t: jnp.pad(t, ((0, 0), (0, 16 - Td), (0, 0)))
    ctx_s = _sample_attn(q_s, pad16(lat_s), pad16(kr_s), cache_latent[l], cache_krope[l].transpose(0, 2, 1),
                         page_table)
    ctx_s = ctx_s.reshape(S, MLA_HEADS, Td, KV_RANK).transpose(1, 2, 0, 3).reshape(1, MLA_HEADS, Td * S, KV_RANK)
    y_s = _layer_back(xs, mod_s, o_rs, gate_s, ctx_s, wp)
    y_s = y_s.reshape(Td, S, D).transpose(1, 0, 2)

    return (y_p, y_s, lat_p[None], kr_p[None], wkv_p[None], last_p[None],
            lat_s[None], kr_s[None], wkv_s[None], last_s[None])
```

```python
import functools

import jax
import jax.numpy as jnp
from jax import lax
from jax.experimental import pallas as pl
from jax.experimental.pallas import tpu as pltpu

F32 = jnp.float32
BF16 = jnp.bfloat16

D_MODEL = 2048
RWKV_WIDTH = 1024
RWKV_HEAD = 64
RWKV_HEADS = 16
DECAY_LORA = 96
AAA_LORA = 96
GATE_LORA = 256
MLA_V = 128
MLA_HEADS = 8
MLA_WIDTH = 1024
MLA_NOPE = 128
MLA_ROPE = 64
Q_RANK = 512
KV_RANK = 256
D_FF = 4 * D_MODEL
ROPE_THETA = 10000.0
N_SHIFT = 3 * RWKV_WIDTH + DECAY_LORA + AAA_LORA + GATE_LORA
N_IN = N_SHIFT + Q_RANK + KV_RANK + MLA_ROPE
DEPTH = 1
DEEPNORM_ALPHA = (2.0 * DEPTH) ** 0.25
LN_EPS = 1e-5
RMS_EPS = 1e-6
GN_EPS = 64e-5
ATTN_SCALE = (MLA_NOPE + MLA_ROPE) ** -0.5
NEG_BIG = -1e30

LANES = 128
VMEM_LIMIT = 56 * 1024 * 1024

OFF_R, OFF_K, OFF_V = 0, RWKV_WIDTH, 2 * RWKV_WIDTH
OFF_GD = 3 * RWKV_WIDTH
OFF_WA = OFF_GD + GATE_LORA
RW_COLS = OFF_WA + 256
OFF_CQ = RW_COLS
OFF_CKV = OFF_CQ + Q_RANK
OFF_KR = OFF_CKV + KV_RANK
NP_COLS = 4608
MLA_BLK = 512


def _perm_cols(a):
    z = lambda n: jnp.zeros(a.shape[:-1] + (n,), a.dtype)
    o_wd = 3 * RWKV_WIDTH
    o_gd = o_wd + DECAY_LORA + AAA_LORA
    pieces = [a[..., :o_wd], a[..., o_gd:N_SHIFT], a[..., o_wd:o_gd], z(256 - DECAY_LORA - AAA_LORA)]
    if a.shape[-1] == N_IN:
        pieces += [a[..., N_SHIFT:N_IN], z(NP_COLS - OFF_KR - MLA_ROPE)]
    return jnp.concatenate(pieces, -1)


def _unperm_shift_cols(a):
    return jnp.concatenate([a[..., :OFF_GD], a[..., OFF_WA:OFF_WA + DECAY_LORA + AAA_LORA],
                            a[..., OFF_GD:OFF_WA]], -1)


def _params(**kw):
    return pltpu.CompilerParams(vmem_limit_bytes=VMEM_LIMIT, **kw)


def _sigmoid(x):
    return 1.0 / (1.0 + jnp.exp(-x))


def _layernorm(x, g, b):
    mu = jnp.mean(x, -1, keepdims=True)
    xc = x - mu
    var = jnp.mean(xc * xc, -1, keepdims=True)
    return xc * lax.rsqrt(var + LN_EPS) * g + b


def _rmsnorm(x, g):
    return x * lax.rsqrt(jnp.mean(x * x, -1, keepdims=True) + RMS_EPS) * g


def _ada_body(c_ref, w_ref, b_ref, o_ref):
    c = c_ref[...]
    a = (c * _sigmoid(c)).astype(BF16)
    o_ref[...] = jnp.dot(a, w_ref[...].astype(BF16), preferred_element_type=F32) + b_ref[...]


def _ada(c, w_ada, b_ada):
    rows, d = c.shape
    n = w_ada.shape[1]
    tn = 1024
    return pl.pallas_call(
        _ada_body,
        grid=(n // tn,),
        in_specs=[pl.BlockSpec((rows, d), lambda j: (0, 0)),
                  pl.BlockSpec((d, tn), lambda j: (0, j)),
                  pl.BlockSpec((1, tn), lambda j: (0, j))],
        out_specs=pl.BlockSpec((rows, tn), lambda j: (0, j)),
        out_shape=jax.ShapeDtypeStruct((rows, n), F32),
        compiler_params=_params(dimension_semantics=("parallel",)),
        name="ada_mod",
    )(c, w_ada, b_ada.reshape(1, n))


def _grouped_spec(tm, d, col=0):
    return pl.BlockSpec((None, tm, d), lambda b, r, *_: (b, r, col))


def _time_major_spec(tm, d):
    return pl.BlockSpec((tm, d), lambda b, r, *_: (r, b))


def _mod_spec(mod, tm, chunk):
    if mod.shape[1] == 1:
        return pl.BlockSpec((None, 1, D_MODEL), lambda b, r, *_: (b, 0, chunk))
    return pl.BlockSpec((None, tm, D_MODEL), lambda b, r, *_: (b, r, chunk))


def _const_spec(shape):
    nd = len(shape)
    return pl.BlockSpec(shape, lambda *_: (0,) * nd)


def _inproj_body(x_ref, sc_ref, sh_ref, w_ref, o_ref, h_scr):
    @pl.when(pl.program_id(2) == 0)
    def _():
        h = x_ref[...] * (1.0 + sc_ref[...]) + sh_ref[...]
        h_scr[...] = h.astype(BF16)

    o_ref[...] = jnp.dot(h_scr[...], w_ref[...], preferred_element_type=F32)


def _in_proj(x, mod, w_in_p):
    B, T, D = x.shape
    tm = min(512, T)
    nj = 2
    tn = NP_COLS // nj
    return pl.pallas_call(
        _inproj_body,
        grid=(B, T // tm, nj),
        in_specs=[_grouped_spec(tm, D), _mod_spec(mod, tm, 1), _mod_spec(mod, tm, 0),
                  pl.BlockSpec((D, tn), lambda b, r, j: (0, j))],
        out_specs=pl.BlockSpec((None, tm, tn), lambda b, r, j: (b, r, j)),
        out_shape=jax.ShapeDtypeStruct((B, T, NP_COLS), F32),
        scratch_shapes=[pltpu.VMEM((tm, D), BF16)],
        compiler_params=_params(dimension_semantics=("parallel", "parallel", "arbitrary")),
        name="in_proj",
    )(x, mod, mod, w_in_p)


SUBLANES = 8


def _rwkv_pre_body(p_ref, pb_ref, prev_ref, mu_ref, wa0_ref, wl_ref, g2_ref,
                   r_ref, k_ref, v_ref, w_ref, a_ref, g_ref, *, shift):
    p = p_ref[...]
    at_start = pl.program_id(1) == 0
    if shift == 1:
        first = jnp.where(at_start, prev_ref[...], pb_ref[SUBLANES - 1:SUBLANES, :])
        row = lax.broadcasted_iota(jnp.int32, p.shape, 0)
        shifted = jnp.where(row == 0, first, pltpu.roll(p, 1, axis=0))
    else:
        first = jnp.where(at_start, prev_ref[...], pb_ref[...])
        shifted = jnp.concatenate([first, p[:-shift]], axis=0)
    z = p + mu_ref[...] * (shifted - p)
    r_ref[...] = z[:, OFF_R:OFF_R + RWKV_WIDTH].T
    k_ref[...] = z[:, OFF_K:OFF_K + RWKV_WIDTH].T
    v_ref[...] = z[:, OFF_V:OFF_V + RWKV_WIDTH].T
    wa = z[:, OFF_WA:OFF_WA + 256]
    lane = lax.broadcasted_iota(jnp.int32, wa.shape, 1)
    lora_in = jnp.where(lane < DECAY_LORA, jnp.tanh(wa), wa).astype(BF16)
    pre = jnp.dot(lora_in, wl_ref[...], preferred_element_type=F32) + wa0_ref[...]
    u = pre[:, :RWKV_WIDTH]
    w_log = jnp.minimum(u, 0.0) - jnp.log1p(jnp.exp(-jnp.abs(u))) - 0.5
    w_ref[...] = jnp.exp(-jnp.exp(w_log)).T
    a_ref[...] = _sigmoid(pre[:, RWKV_WIDTH:]).T
    gd = _sigmoid(z[:, OFF_GD:OFF_GD + GATE_LORA]).astype(BF16)
    g_ref[...] = jnp.dot(gd, g2_ref[...], preferred_element_type=F32)


def _rwkv_pre(proj, prev_p, shift, mu_p, wa0, w_lora, g2):
    B, T, _ = proj.shape
    tm = min(256, T)
    back = max(shift, SUBLANES)
    assert tm % back == 0 and prev_p.shape[1] == shift
    blk = pl.BlockSpec((RWKV_WIDTH, tm), lambda b, r: (b, r))
    out = jax.ShapeDtypeStruct((B * RWKV_WIDTH, T), F32)
    return pl.pallas_call(
        functools.partial(_rwkv_pre_body, shift=shift),
        grid=(B, T // tm),
        in_specs=[_grouped_spec(tm, RW_COLS),
                  pl.BlockSpec((None, back, RW_COLS), lambda b, r: (b, jnp.maximum(r * (tm // back) - 1, 0), 0)),
                  pl.BlockSpec((None, shift, RW_COLS), lambda b, r: (b, 0, 0)),
                  _const_spec((1, RW_COLS)), _const_spec((1, 2 * RWKV_WIDTH)),
                  _const_spec((256, 2 * RWKV_WIDTH)), _const_spec((GATE_LORA, RWKV_WIDTH))],
        out_specs=[blk] * 5 + [_grouped_spec(tm, RWKV_WIDTH)],
        out_shape=[out] * 5 + [jax.ShapeDtypeStruct((B, T, RWKV_WIDTH), F32)],
        compiler_params=_params(dimension_semantics=("parallel", "parallel")),
        name="rwkv_pre",
    )(proj, proj, prev_p, mu_p, wa0, w_lora, g2)


WKV_UNROLL = 8


def _wkv_body(r_ref, k_ref, v_ref, w_ref, a_ref, kk_p, ka_p, rk_p, lg_p, lb_p, s0_ref,
              o_ref, sout_ref, S, b_kk, b_wr, b_kka, b_k2, b_rows):
    tt = r_ref.shape[0]
    N = RWKV_HEAD

    @pl.when(pl.program_id(1) == 0)
    def _():
        S[...] = s0_ref[...]

    r = r_ref[...]
    k = k_ref[...]
    a = a_ref[...]
    kk = k * kk_p[...]
    nrm = jnp.sqrt(jnp.sum(kk * kk, axis=1, keepdims=True))
    kk = kk / jnp.maximum(nrm, 1e-12)
    k2 = k * (1.0 + (a - 1.0) * ka_p[...])
    kka = kk * a
    b_kk[...] = kk
    b_wr[...] = w_ref[...] * r
    b_kka[...] = kka
    b_k2[...] = k2
    b_rows[:, 0:1, :] = jnp.sum(kka * r, axis=1, keepdims=True)
    b_rows[:, 1:2, :] = jnp.sum(k2 * r, axis=1, keepdims=True)
    b_rows[:, 2:3, :] = jnp.sum(r * k2 * rk_p[...], axis=1, keepdims=True)

    def tok(t, carry):
        def reduce_pass(j0, acc):
            sa, o1 = acc
            for jj in range(WKV_UNROLL):
                j = j0 * WKV_UNROLL + jj
                Sj = S[j]
                sa = sa + Sj * b_kk[t, pl.ds(j, 1), :]
                o1 = o1 + Sj * b_wr[t, pl.ds(j, 1), :]
            return sa, o1

        zero = jnp.zeros((N, LANES), F32)
        sa, o1 = lax.fori_loop(0, N // WKV_UNROLL, reduce_pass, (zero, zero))
        vT = v_ref[t]
        o_ref[t] = o1 - sa * b_rows[t, 0:1, :] + vT * b_rows[t, 1:2, :]

        def update_pass(j0, c):
            for jj in range(WKV_UNROLL):
                j = j0 * WKV_UNROLL + jj
                S[j] = (S[j] * w_ref[t, pl.ds(j, 1), :] - sa * b_kka[t, pl.ds(j, 1), :]
                        + vT * b_k2[t, pl.ds(j, 1), :])
            return c

        lax.fori_loop(0, N // WKV_UNROLL, update_pass, 0)
        return carry

    lax.fori_loop(0, tt, tok, 0)

    o = o_ref[...]
    mu = jnp.mean(o, axis=1, keepdims=True)
    oc = o - mu
    var = jnp.mean(oc * oc, axis=1, keepdims=True)
    on = oc * lax.rsqrt(var + GN_EPS) * lg_p[...] + lb_p[...]
    o_ref[...] = on + b_rows[:, 2:3, :] * v_ref[...]

    @pl.when(pl.program_id(1) == pl.num_programs(1) - 1)
    def _():
        sout_ref[...] = S[...]


def _wkv(r, k, v, w, a, tiles, s0):
    T, G, N, _ = r.shape
    tt = min(32, T)
    tok_spec = pl.BlockSpec((tt, None, N, LANES), lambda g_, i: (i, g_, 0, 0))
    st_spec = pl.BlockSpec((None, N, N, LANES), lambda g_, i: (g_, 0, 0, 0))
    per_group = tiles[0].shape[0] > 1
    tile_spec = pl.BlockSpec((None, N, LANES), lambda g_, i: (g_ if per_group else 0, 0, 0))
    return pl.pallas_call(
        _wkv_body,
        grid=(G, T // tt),
        in_specs=[tok_spec] * 5 + [tile_spec] * 5 + [st_spec],
        out_specs=[tok_spec, st_spec],
        out_shape=[jax.ShapeDtypeStruct((T, G, N, LANES), F32), jax.ShapeDtypeStruct((G, N, N, LANES), F32)],
        scratch_shapes=[pltpu.VMEM((N, N, LANES), F32)] + [pltpu.VMEM((tt, N, LANES), F32)] * 4
                       + [pltpu.VMEM((tt, SUBLANES, LANES), F32)],
        compiler_params=_params(dimension_semantics=("parallel", "arbitrary")),
        name="wkv_scan",
    )(r, k, v, w, a, *tiles, s0)


def _rot_half(x):
    n = x.shape[-1]
    lane = lax.broadcasted_iota(jnp.int32, x.shape, x.ndim - 1)
    fwd = pltpu.roll(x, n - MLA_ROPE // 2, axis=x.ndim - 1)
    bwd = pltpu.roll(x, MLA_ROPE // 2, axis=x.ndim - 1)
    return jnp.where(lane % MLA_ROPE < MLA_ROPE // 2, fwd, bwd)


def _mla_body(cq_ref, ckr_ref, cos_ref, sin_ref, qn_ref, wq_ref, wuk_ref, kvn_ref,
              q_ref, kc_ref, lat_ref, kr_ref, latt_ref):
    cq = cq_ref[...]
    q = jnp.dot(_rmsnorm(cq, qn_ref[...]).astype(BF16), wq_ref[...], preferred_element_type=F32)
    nope_w = MLA_HEADS * MLA_NOPE
    qr = q[:, nope_w:]
    q_rope = qr * cos_ref[...] + _rot_half(qr) * sin_ref[...]
    for h in range(MLA_HEADS):
        qn = q[:, h * MLA_NOPE:(h + 1) * MLA_NOPE].astype(BF16)
        q_ref[h, :, :KV_RANK] = jnp.dot(qn, wuk_ref[h], preferred_element_type=F32).astype(BF16)
        q_ref[h, :, KV_RANK:] = q_rope[:, h * MLA_ROPE:(h + 1) * MLA_ROPE].astype(BF16)
    ckr = ckr_ref[...]
    lat = _rmsnorm(ckr[:, :KV_RANK], kvn_ref[...])
    slab = ckr[:, KV_RANK:KV_RANK + LANES]
    kr = (slab * cos_ref[:, :LANES] + _rot_half(slab) * sin_ref[:, :LANES])[:, :MLA_ROPE]
    lat_ref[...] = lat
    kr_ref[...] = kr
    kc_ref[:, :KV_RANK] = lat.astype(BF16)
    kc_ref[:, KV_RANK:] = kr.astype(BF16)
    latt_ref[...] = lat.T.astype(BF16)


def _mla_proj(proj, cos_t, sin_t, q_norm, wq_p, wuk_t, kv_norm):
    B, T, _ = proj.shape
    tm = min(512, T)
    qk = KV_RANK + MLA_ROPE
    tab = pl.BlockSpec((None, tm, MLA_HEADS * MLA_ROPE), lambda b, r: (0, r, 0))
    return pl.pallas_call(
        _mla_body,
        grid=(B, T // tm),
        in_specs=[_grouped_spec(tm, MLA_BLK, OFF_CQ // MLA_BLK), _grouped_spec(tm, MLA_BLK, OFF_CKV // MLA_BLK),
                  tab, tab, _const_spec((1, Q_RANK)), _const_spec(wq_p.shape), _const_spec(wuk_t.shape),
                  _const_spec((1, KV_RANK))],
        out_specs=[pl.BlockSpec((None, MLA_HEADS, tm, qk), lambda b, r: (b, 0, r, 0)),
                   _grouped_spec(tm, qk), _grouped_spec(tm, KV_RANK), _grouped_spec(tm, MLA_ROPE),
                   pl.BlockSpec((None, KV_RANK, tm), lambda b, r: (b, 0, r))],
        out_shape=[jax.ShapeDtypeStruct((B, MLA_HEADS, T, qk), BF16),
                   jax.ShapeDtypeStruct((B, T, qk), BF16),
                   jax.ShapeDtypeStruct((B, T, KV_RANK), F32),
                   jax.ShapeDtypeStruct((B, T, MLA_ROPE), F32),
                   jax.ShapeDtypeStruct((B, KV_RANK, T), BF16)],
        compiler_params=_params(dimension_semantics=("parallel", "parallel")),
        name="mla_proj",
    )(proj, proj, cos_t, sin_t, q_norm, wq_p, wuk_t, kv_norm)


HEADS_PER_DOT = 4


def _pattn_body(q_ref, k_ref, vt_ref, o_ref, m_scr, l_scr, acc_scr, *, tq, tk):
    i = pl.program_id(1)
    j = pl.program_id(2)
    last_j = ((i + 1) * tq - 1) // tk

    @pl.when(j == 0)
    def _():
        m_scr[...] = jnp.full(m_scr.shape, NEG_BIG, F32)
        l_scr[...] = jnp.zeros(l_scr.shape, F32)
        acc_scr[...] = jnp.zeros(acc_scr.shape, F32)

    def step(masked):
        k = k_ref[...]
        vt = vt_ref[...]
        if masked:
            cols = HEADS_PER_DOT * tq
            kpos = j * tk + lax.broadcasted_iota(jnp.int32, (tk, cols), 0)
            qpos = i * tq + lax.broadcasted_iota(jnp.int32, (tk, cols), 1) % tq
            keep = kpos <= qpos
        for g in range(MLA_HEADS // HEADS_PER_DOT):
            q = q_ref[g * HEADS_PER_DOT:(g + 1) * HEADS_PER_DOT].reshape(HEADS_PER_DOT * tq, q_ref.shape[-1])
            st = lax.dot_general(k, q, (((1,), (1,)), ((), ())),
                                 preferred_element_type=F32) * ATTN_SCALE
            if masked:
                st = jnp.where(keep, st, NEG_BIG)
            m_prev = m_scr[g]
            m_new = jnp.maximum(m_prev, jnp.max(st, axis=0, keepdims=True))
            alpha = jnp.exp(m_prev - m_new)
            pt = jnp.exp(st - m_new)
            l_scr[g] = alpha * l_scr[g] + jnp.sum(pt, axis=0, keepdims=True)
            acc_scr[g] = alpha * acc_scr[g] + jnp.dot(vt, pt.astype(BF16), preferred_element_type=F32)
            m_scr[g] = m_new

    crosses_diagonal = (j + 1) * tk - 1 > i * tq
    pl.when(jnp.logical_and(j <= last_j, crosses_diagonal))(functools.partial(step, True))
    pl.when(jnp.logical_and(j <= last_j, jnp.logical_not(crosses_diagonal)))(functools.partial(step, False))

    @pl.when(j == pl.num_programs(2) - 1)
    def _():
        for g in range(MLA_HEADS // HEADS_PER_DOT):
            ctx_t = acc_scr[g] / l_scr[g]
            for hh in range(HEADS_PER_DOT):
                o_ref[g * HEADS_PER_DOT + hh] = ctx_t[:, hh * tq:(hh + 1) * tq].T.astype(BF16)


def _prompt_attn(q4, kc, lat_t):
    B, H, T, qk = q4.shape
    tq = min(256, T)
    tk = min(512, T)
    body = functools.partial(_pattn_body, tq=tq, tk=tk)
    kv_block = lambda i, j: jnp.minimum(j, ((i + 1) * tq - 1) // tk)
    return pl.pallas_call(
        body,
        grid=(B, T // tq, T // tk),
        in_specs=[pl.BlockSpec((None, H, tq, qk), lambda b, i, j: (b, 0, i, 0)),
                  pl.BlockSpec((None, tk, qk), lambda b, i, j: (b, kv_block(i, j), 0)),
                  pl.BlockSpec((None, KV_RANK, tk), lambda b, i, j: (b, 0, kv_block(i, j)))],
        out_specs=pl.BlockSpec((None, H, tq, KV_RANK), lambda b, i, j: (b, 0, i, 0)),
        out_shape=jax.ShapeDtypeStruct((B, H, T, KV_RANK), BF16),
        scratch_shapes=[pltpu.VMEM((H // HEADS_PER_DOT, 1, HEADS_PER_DOT * tq), F32),
                        pltpu.VMEM((H // HEADS_PER_DOT, 1, HEADS_PER_DOT * tq), F32),
                        pltpu.VMEM((H // HEADS_PER_DOT, KV_RANK, HEADS_PER_DOT * tq), F32)],
        compiler_params=_params(dimension_semantics=("parallel", "parallel", "arbitrary")),
        name="prompt_attn",
    )(q4, kc, lat_t)


SAMPLE_PAGES_PER_CHUNK = 64


def _sattn_body(pt_ref, q_ref, latn_ref, krn_ref, lat_hbm, kr_hbm, o_ref,
                lat_buf, kr_buf, sem, m_scr, l_scr, acc_scr, *, n_pg, n_chunks, n_new):
    seq = pl.program_id(0)
    nt = (((1,), (1,)), ((), ()))

    def page_copies(page_of, slot):
        for p in range(n_pg):
            page = page_of(p)
            yield pltpu.make_async_copy(lat_hbm.at[page], lat_buf.at[slot, p], sem.at[slot, 0])
            yield pltpu.make_async_copy(kr_hbm.at[page], kr_buf.at[slot, p], sem.at[slot, 1])

    def start_chunk(s, chunk, slot):
        for cp in page_copies(lambda p: pt_ref[s, chunk * n_pg + p], slot):
            cp.start()

    def wait_chunk(slot):
        for cp in page_copies(lambda p: 0, slot):
            cp.wait()

    @pl.when(seq == 0)
    def _():
        start_chunk(seq, 0, 0)

    m_scr[...] = jnp.full(m_scr.shape, NEG_BIG, F32)
    l_scr[...] = jnp.zeros(l_scr.shape, F32)
    acc_scr[...] = jnp.zeros(acc_scr.shape, F32)

    q = q_ref[...]
    q_lat = q[:, :KV_RANK]
    q_rope = q[:, KV_RANK:]

    def update(s, vals):
        m_prev = m_scr[...]
        m_new = jnp.maximum(m_prev, jnp.max(s, axis=1, keepdims=True))
        alpha = jnp.exp(m_prev - m_new)
        p = jnp.exp(s - m_new)
        l_scr[...] = alpha * l_scr[...] + jnp.sum(p, axis=1, keepdims=True)
        acc_scr[...] = alpha * acc_scr[...] + jnp.dot(p.astype(BF16), vals, preferred_element_type=F32)
        m_scr[...] = m_new

    for chunk in range(n_chunks):
        slot = chunk % 2
        if chunk + 1 < n_chunks:
            start_chunk(seq, chunk + 1, 1 - slot)
        else:
            @pl.when(seq + 1 < pl.num_programs(0))
            def _():
                start_chunk(seq + 1, 0, 1 - slot)
        wait_chunk(slot)
        lat_all = lat_buf[slot].reshape(n_pg * lat_buf.shape[2], KV_RANK).astype(BF16)
        krt_all = jnp.concatenate([kr_buf[slot, p].astype(BF16) for p in range(n_pg)], axis=1)
        s = lax.dot_general(q_lat, lat_all, nt, preferred_element_type=F32)
        s = (s + jnp.dot(q_rope, krt_all, preferred_element_type=F32)) * ATTN_SCALE
        update(s, lat_all)

    latn = latn_ref[...].astype(BF16)
    sn = lax.dot_general(q_lat, latn, nt, preferred_element_type=F32)
    sn = (sn + lax.dot_general(q_rope, krn_ref[...].astype(BF16), nt, preferred_element_type=F32)) * ATTN_SCALE
    row_t = lax.broadcasted_iota(jnp.int32, sn.shape, 0) % n_new
    col_t = lax.broadcasted_iota(jnp.int32, sn.shape, 1)
    sn = jnp.where(col_t <= row_t, sn, NEG_BIG)
    update(sn, latn)
    o_ref[...] = acc_scr[...] / l_scr[...]


def _sample_attn(q, latn, krn, cache_lat, cache_kr, page_table):
    S, rows, qk = q.shape
    n_pages = page_table.shape[1]
    page = cache_lat.shape[1]
    n_pg = min(SAMPLE_PAGES_PER_CHUNK, n_pages // 2)
    n_chunks = n_pages // n_pg
    assert n_pages % n_pg == 0 and n_chunks % 2 == 0
    n_new = rows // MLA_HEADS
    pad_new = latn.shape[1]
    grid_spec = pltpu.PrefetchScalarGridSpec(
        num_scalar_prefetch=1,
        grid=(S,),
        in_specs=[pl.BlockSpec((None, rows, qk), lambda s, pt: (s, 0, 0)),
                  pl.BlockSpec((None, pad_new, KV_RANK), lambda s, pt: (s, 0, 0)),
                  pl.BlockSpec((None, pad_new, MLA_ROPE), lambda s, pt: (s, 0, 0)),
                  pl.BlockSpec(memory_space=pl.ANY), pl.BlockSpec(memory_space=pl.ANY)],
        out_specs=pl.BlockSpec((None, rows, KV_RANK), lambda s, pt: (s, 0, 0)),
        scratch_shapes=[pltpu.VMEM((2, n_pg, page, KV_RANK), F32), pltpu.VMEM((2, n_pg, MLA_ROPE, page), F32),
                        pltpu.SemaphoreType.DMA((2, 2)),
                        pltpu.VMEM((rows, 1), F32), pltpu.VMEM((rows, 1), F32),
                        pltpu.VMEM((rows, KV_RANK), F32)],
    )
    return pl.pallas_call(
        functools.partial(_sattn_body, n_pg=n_pg, n_chunks=n_chunks, n_new=n_new),
        grid_spec=grid_spec,
        out_shape=jax.ShapeDtypeStruct((S, rows, KV_RANK), F32),
        compiler_params=_params(dimension_semantics=("arbitrary",)),
        name="sample_attn",
    )(page_table, q, latn, krn, cache_lat, cache_kr)


def _uv_body(ctx_ref, wuv_ref, on_ref, o_ref):
    parts = [jnp.dot(ctx_ref[h].astype(BF16), wuv_ref[h], preferred_element_type=F32)
             for h in range(MLA_HEADS)]
    om = jnp.concatenate(parts, axis=1)
    o_ref[...] = _rmsnorm(om, on_ref[...]).astype(BF16)


def _uv_norm(ctx, wuv_t, out_norm):
    B, H, T, R = ctx.shape
    tm = min(512, T)
    return pl.pallas_call(
        _uv_body,
        grid=(B, T // tm),
        in_specs=[pl.BlockSpec((None, H, tm, R), lambda b, r: (b, 0, r, 0)),
                  _const_spec(wuv_t.shape), _const_spec((1, MLA_WIDTH))],
        out_specs=_grouped_spec(tm, MLA_WIDTH),
        out_shape=jax.ShapeDtypeStruct((B, T, MLA_WIDTH), BF16),
        compiler_params=_params(dimension_semantics=("parallel", "parallel")),
        name="uv_norm",
    )(ctx, wuv_t, out_norm)


def _outproj_body(or_ref, gate_ref, om_ref, x_ref, g1_ref, w_ref, lg_ref, lb_ref, o_ref):
    o_r = (or_ref[...] * gate_ref[...]).astype(BF16)
    mixed = jnp.dot(o_r, w_ref[:RWKV_WIDTH, :], preferred_element_type=F32)
    mixed = mixed + jnp.dot(om_ref[...], w_ref[RWKV_WIDTH:, :], preferred_element_type=F32)
    y = DEEPNORM_ALPHA * x_ref[...] + g1_ref[...] * mixed
    o_ref[...] = _layernorm(y, lg_ref[...], lb_ref[...])


def _out_proj(o_r, gate, o_m, x, mod, w_out, ln_g, ln_b):
    B, T, D = x.shape
    tm = min(256, T)
    return pl.pallas_call(
        _outproj_body,
        grid=(B, T // tm),
        in_specs=[_time_major_spec(tm, RWKV_WIDTH), _grouped_spec(tm, RWKV_WIDTH),
                  _grouped_spec(tm, MLA_WIDTH), _grouped_spec(tm, D),
                  _mod_spec(mod, tm, 2), _const_spec((D, D)), _const_spec((1, D)), _const_spec((1, D))],
        out_specs=_grouped_spec(tm, D),
        out_shape=jax.ShapeDtypeStruct((B, T, D), F32),
        compiler_params=_params(dimension_semantics=("parallel", "parallel")),
        name="out_proj_ln1",
    )(o_r, gate, o_m, x, mod, w_out, ln_g, ln_b)


def _ffn_body(x_ref, sc_ref, sh_ref, g2_ref, wu_ref, wd_ref, lg_ref, lb_ref, o_ref, h_scr, acc_scr):
    f = pl.program_id(2)

    @pl.when(f == 0)
    def _():
        h_scr[...] = (x_ref[...] * (1.0 + sc_ref[...]) + sh_ref[...]).astype(BF16)
        acc_scr[...] = jnp.zeros(acc_scr.shape, F32)

    u = jnp.maximum(jnp.dot(h_scr[...], wu_ref[...], preferred_element_type=F32), 0.0)
    acc_scr[...] += jnp.dot((u * u).astype(BF16), wd_ref[...], preferred_element_type=F32)

    @pl.when(f == pl.num_programs(2) - 1)
    def _():
        y = DEEPNORM_ALPHA * x_ref[...] + g2_ref[...] * acc_scr[...]
        o_ref[...] = _layernorm(y, lg_ref[...], lb_ref[...])


def _ffn(x, mod, w_up, w_down, ln_g, ln_b):
    B, T, D = x.shape
    tm = min(512, T)
    tf = 1024
    return pl.pallas_call(
        _ffn_body,
        grid=(B, T // tm, D_FF // tf),
        in_specs=[_grouped_spec(tm, D), _mod_spec(mod, tm, 4), _mod_spec(mod, tm, 3), _mod_spec(mod, tm, 5),
                  pl.BlockSpec((D, tf), lambda b, r, f: (0, f)),
                  pl.BlockSpec((tf, D), lambda b, r, f: (f, 0)),
                  _const_spec((1, D)), _const_spec((1, D))],
        out_specs=_grouped_spec(tm, D),
        out_shape=jax.ShapeDtypeStruct((B, T, D), F32),
        scratch_shapes=[pltpu.VMEM((tm, D), BF16), pltpu.VMEM((tm, D), F32)],
        compiler_params=_params(dimension_semantics=("parallel", "parallel", "arbitrary")),
        name="ffn_ln2",
    )(x, mod, mod, mod, w_up, w_down, ln_g, ln_b)


def _rope_tables(pos):
    half = MLA_ROPE // 2
    inv = ROPE_THETA ** (-jnp.arange(half, dtype=F32) / half)
    ang = pos.astype(F32)[:, None] * inv
    cos, sin = jnp.cos(ang), jnp.sin(ang)
    cos_t = jnp.tile(jnp.concatenate([cos, cos], -1), (1, MLA_HEADS))
    sin_t = jnp.tile(jnp.concatenate([-sin, sin], -1), (1, MLA_HEADS))
    return cos_t[None], sin_t[None]


SEQ_PER_GROUP = LANES // RWKV_HEADS


class _PromptLanes:
    @staticmethod
    def to_lanes(x, n_seq):
        T = x.shape[1]
        return x.reshape(n_seq // SEQ_PER_GROUP, LANES, RWKV_HEAD, T).transpose(3, 0, 2, 1)

    @staticmethod
    def from_lanes(o):
        T, G = o.shape[:2]
        return o.transpose(0, 1, 3, 2).reshape(T, G * LANES * RWKV_HEAD)

    @staticmethod
    def tile(p, n_seq):
        return jnp.tile(p.reshape(RWKV_HEADS, RWKV_HEAD).T, (1, SEQ_PER_GROUP))[None]

    @staticmethod
    def state_to_lanes(s):
        B = s.shape[0]
        s = s.reshape(B // SEQ_PER_GROUP, LANES, RWKV_HEAD, RWKV_HEAD)
        return s.transpose(0, 3, 2, 1)

    @staticmethod
    def state_from_lanes(s):
        G = s.shape[0]
        return s.transpose(0, 3, 2, 1).reshape(G * SEQ_PER_GROUP, RWKV_HEADS, RWKV_HEAD, RWKV_HEAD)


class _SampleLanes:
    @staticmethod
    def to_lanes(x, n_seq):
        x = x.reshape(RWKV_HEADS, RWKV_HEAD, -1, n_seq // LANES, LANES)
        return x.transpose(2, 3, 0, 1, 4).reshape(x.shape[2], (n_seq // LANES) * RWKV_HEADS, RWKV_HEAD, LANES)

    @staticmethod
    def from_lanes(o):
        Td, G = o.shape[:2]
        o = o.reshape(Td, G // RWKV_HEADS, RWKV_HEADS, RWKV_HEAD, LANES)
        return o.transpose(0, 1, 4, 2, 3).reshape(Td * (G // RWKV_HEADS) * LANES, RWKV_WIDTH)

    @staticmethod
    def tile(p, n_seq):
        t = jnp.broadcast_to(p.reshape(RWKV_HEADS, RWKV_HEAD, 1), (RWKV_HEADS, RWKV_HEAD, LANES))
        return jnp.tile(t, (n_seq // LANES, 1, 1))

    @staticmethod
    def state_to_lanes(s):
        S = s.shape[0]
        s = s.reshape(S // LANES, LANES, RWKV_HEADS, RWKV_HEAD, RWKV_HEAD)
        return s.transpose(0, 2, 4, 3, 1).reshape((S // LANES) * RWKV_HEADS, RWKV_HEAD, RWKV_HEAD, LANES)

    @staticmethod
    def state_from_lanes(s):
        G = s.shape[0]
        s = s.reshape(G // RWKV_HEADS, RWKV_HEADS, RWKV_HEAD, RWKV_HEAD, LANES)
        return s.transpose(0, 4, 1, 3, 2).reshape((G // RWKV_HEADS) * LANES, RWKV_HEADS, RWKV_HEAD, RWKV_HEAD)


def _rwkv_inputs(proj, prev_p, n_seq, lanes, wp):
    r, k, v, w, a, g = _rwkv_pre(proj, prev_p, prev_p.shape[1], wp['mu'], wp['wa0'], wp['w_lora'], wp['g2'])
    return tuple(lanes.to_lanes(t, n_seq) for t in (r, k, v, w, a)), g


def _rwkv_scan(lane_inputs, state, lanes, wp):
    tiles = [lanes.tile(p, state.shape[0]) for p in wp['head_params']]
    o, s_out = _wkv(*lane_inputs, tiles, lanes.state_to_lanes(state))
    return lanes.from_lanes(o), lanes.state_from_lanes(s_out)


def _layer_back(x, mod, o_r, gate, ctx, wp):
    o_m = _uv_norm(ctx, wp['wuv'], wp['out_norm'])
    x1 = _out_proj(o_r, gate, o_m, x, mod, wp['w_out'], wp['ln1_g'], wp['ln1_b'])
    return _ffn(x1, mod, wp['w_up'], wp['w_down'], wp['ln2_g'], wp['ln2_b'])


def kernel(x_prompt, x_sample, c_prompt, c_sample, cache_latent, cache_krope, state_wkv, state_shift, page_table, w_ada, b_ada, w_in, rwkv_mu, rwkv_w0, rwkv_w2, rwkv_a0, rwkv_a2, rwkv_g2, rwkv_k_k, rwkv_k_a, rwkv_r_k, rwkv_lnx_g, rwkv_lnx_b, mla_q_norm, mla_w_q_up, mla_kv_norm, mla_w_uk, mla_w_uv, mla_out_norm, w_out, ln1_g, ln1_b, w_up, w_down, ln2_g, ln2_b):
    B, T, D = x_prompt.shape
    S, Td, _ = x_sample.shape
    past = page_table.shape[1] * cache_latent.shape[2]
    l = 0

    w_lora = jnp.zeros((256, 2 * RWKV_WIDTH), F32)
    w_lora = w_lora.at[:DECAY_LORA, :RWKV_WIDTH].set(rwkv_w2[l])
    w_lora = w_lora.at[DECAY_LORA:DECAY_LORA + AAA_LORA, RWKV_WIDTH:].set(rwkv_a2[l])
    wq = mla_w_q_up[l].reshape(Q_RANK, MLA_HEADS, MLA_NOPE + MLA_ROPE)
    wq_p = jnp.concatenate([wq[:, :, :MLA_NOPE].reshape(Q_RANK, -1), wq[:, :, MLA_NOPE:].reshape(Q_RANK, -1)], -1)
    row = lambda p: p.reshape(1, -1)
    wp = dict(
        w_in=_perm_cols(w_in[l]).astype(BF16),
        mu=row(_perm_cols(rwkv_mu[l])),
        wa0=row(jnp.concatenate([rwkv_w0[l], rwkv_a0[l]])),
        w_lora=w_lora.astype(BF16),
        g2=rwkv_g2[l].astype(BF16),
        head_params=[rwkv_k_k[l], rwkv_k_a[l], rwkv_r_k[l].reshape(-1), rwkv_lnx_g[l], rwkv_lnx_b[l]],
        q_norm=row(mla_q_norm[l]), wq=wq_p.astype(BF16),
        wuk=mla_w_uk[l].transpose(1, 2, 0).astype(BF16),
        kv_norm=row(mla_kv_norm[l]),
        wuv=mla_w_uv[l].transpose(1, 0, 2).astype(BF16),
        out_norm=row(mla_out_norm[l]),
        w_out=w_out[l].astype(BF16), ln1_g=row(ln1_g[l]), ln1_b=row(ln1_b[l]),
        w_up=w_up[l].astype(BF16), w_down=w_down[l].astype(BF16), ln2_g=row(ln2_g[l]), ln2_b=row(ln2_b[l]),
    )

    mod = _ada(jnp.concatenate([c_prompt, c_sample], 0), w_ada[l], b_ada[l])
    mod_p = mod[:B, None, :]
    mod_s = jnp.tile(mod[B:], (Td, 1))[None]

    s0_p = jnp.zeros((B, RWKV_HEADS, RWKV_HEAD, RWKV_HEAD), F32)
    prev0 = jnp.zeros((B, 1, RW_COLS), F32)
    proj_p = _in_proj(x_prompt, mod_p, wp['w_in'])
    lane_in_p, gate_p = _rwkv_inputs(proj_p, prev0, B, _PromptLanes, wp)
    cos_t, sin_t = _rope_tables(jnp.arange(T))
    q4, kc, lat_p, kr_p, lat_t = _mla_proj(proj_p, cos_t, sin_t, wp['q_norm'], wp['wq'], wp['wuk'],
                                           wp['kv_norm'])
    last_p = _unperm_shift_cols(proj_p[:, T - 1, :RW_COLS])
    ctx_p = _prompt_attn(q4, kc, lat_t)
    lane_in_p, ctx_p = lax.optimization_barrier((lane_in_p, ctx_p))
    o_r, wkv_p = _rwkv_scan(lane_in_p, s0_p, _PromptLanes, wp)
    y_p = _layer_back(x_prompt, mod_p, o_r, gate_p, ctx_p, wp)

    xs = x_sample.transpose(1, 0, 2).reshape(1, Td * S, D)
    pos_s = jnp.repeat(past + jnp.arange(Td), S)
    prev_s = _perm_cols(state_shift[l])[None]
    proj = _in_proj(xs, mod_s, wp['w_in'])
    lane_in_s, gate_s = _rwkv_inputs(proj, prev_s, S, _SampleLanes, wp)
    o_rs, wkv_s = _rwkv_scan(lane_in_s, state_wkv[l], _SampleLanes, wp)
    cos_t, sin_t = _rope_tables(pos_s)
    q4s, _, lat_s, kr_s, _ = _mla_proj(proj, cos_t, sin_t, wp['q_norm'], wp['wq'], wp['wuk'], wp['kv_norm'])
    last_s = _unperm_shift_cols(proj[0, (Td - 1) * S:, :RW_COLS])
    qk = KV_RANK + MLA_ROPE
    q_s = q4s.reshape(MLA_HEADS, Td, S, qk).transpose(2, 0, 1, 3).reshape(S, MLA_HEADS * Td, qk)
    lat_s = lat_s.reshape(Td, S, KV_RANK).transpose(1, 0, 2)
    kr_s = kr_s.reshape(Td, S, MLA_ROPE).transpose(1, 0, 2)
    pad16 = lambda t: jnp.pad(t, ((0, 0), (0, 16 - Td), (0, 0)))
    ctx_s = _sample_attn(q_s, pad16(lat_s), pad16(kr_s), cache_latent[l], cache_krope[l].transpose(0, 2, 1),
                         page_table)
    ctx_s = ctx_s.reshape(S, MLA_HEADS, Td, KV_RANK).transpose(1, 2, 0, 3).reshape(1, MLA_HEADS, Td * S, KV_RANK)
    y_s = _layer_back(xs, mod_s, o_rs, gate_s, ctx_s, wp)
    y_s = y_s.reshape(Td, S, D).transpose(1, 0, 2)

    return (y_p, y_s, lat_p[None], kr_p[None], wkv_p[None], last_p[None],
            lat_s[None], kr_s[None], wkv_s[None], last_s[None])
```

```python
import functools

import jax
import jax.numpy as jnp
from jax import lax
from jax.experimental import pallas as pl
from jax.experimental.pallas import tpu as pltpu

F32 = jnp.float32
BF16 = jnp.bfloat16

D_MODEL = 2048
RWKV_WIDTH = 1024
RWKV_HEAD = 64
RWKV_HEADS = 16
DECAY_LORA = 96
AAA_LORA = 96
GATE_LORA = 256
MLA_V = 128
MLA_HEADS = 8
MLA_WIDTH = 1024
MLA_NOPE = 128
MLA_ROPE = 64
Q_RANK = 512
KV_RANK = 256
D_FF = 4 * D_MODEL
ROPE_THETA = 10000.0
N_SHIFT = 3 * RWKV_WIDTH + DECAY_LORA + AAA_LORA + GATE_LORA
N_IN = N_SHIFT + Q_RANK + KV_RANK + MLA_ROPE
DEPTH = 1
DEEPNORM_ALPHA = (2.0 * DEPTH) ** 0.25
LN_EPS = 1e-5
RMS_EPS = 1e-6
GN_EPS = 64e-5
ATTN_SCALE = (MLA_NOPE + MLA_ROPE) ** -0.5
NEG_BIG = -1e30

LANES = 128
VMEM_LIMIT = 56 * 1024 * 1024

OFF_R, OFF_K, OFF_V = 0, RWKV_WIDTH, 2 * RWKV_WIDTH
OFF_GD = 3 * RWKV_WIDTH
OFF_WA = OFF_GD + GATE_LORA
RW_COLS = OFF_WA + 256
OFF_CQ = RW_COLS
OFF_CKV = OFF_CQ + Q_RANK
OFF_KR = OFF_CKV + KV_RANK
NP_COLS = 4608
MLA_BLK = 512


def _perm_cols(a):
    z = lambda n: jnp.zeros(a.shape[:-1] + (n,), a.dtype)
    o_wd = 3 * RWKV_WIDTH
    o_gd = o_wd + DECAY_LORA + AAA_LORA
    pieces = [a[..., :o_wd], a[..., o_gd:N_SHIFT], a[..., o_wd:o_gd], z(256 - DECAY_LORA - AAA_LORA)]
    if a.shape[-1] == N_IN:
        pieces += [a[..., N_SHIFT:N_IN], z(NP_COLS - OFF_KR - MLA_ROPE)]
    return jnp.concatenate(pieces, -1)


def _unperm_shift_cols(a):
    return jnp.concatenate([a[..., :OFF_GD], a[..., OFF_WA:OFF_WA + DECAY_LORA + AAA_LORA],
                            a[..., OFF_GD:OFF_WA]], -1)


def _params(**kw):
    return pltpu.CompilerParams(vmem_limit_bytes=VMEM_LIMIT, **kw)


def _sigmoid(x):
    return 1.0 / (1.0 + jnp.exp(-x))


def _layernorm(x, g, b):
    mu = jnp.mean(x, -1, keepdims=True)
    xc = x - mu
    var = jnp.mean(xc * xc, -1, keepdims=True)
    return xc * lax.rsqrt(var + LN_EPS) * g + b


def _rmsnorm(x, g):
    return x * lax.rsqrt(jnp.mean(x * x, -1, keepdims=True) + RMS_EPS) * g


def _ada_body(c_ref, w_ref, b_ref, o_ref):
    c = c_ref[...]
    a = (c * _sigmoid(c)).astype(BF16)
    o_ref[...] = jnp.dot(a, w_ref[...].astype(BF16), preferred_element_type=F32) + b_ref[...]


def _ada(c, w_ada, b_ada):
    rows, d = c.shape
    n = w_ada.shape[1]
    tn = 1024
    return pl.pallas_call(
        _ada_body,
        grid=(n // tn,),
        in_specs=[pl.BlockSpec((rows, d), lambda j: (0, 0)),
                  pl.BlockSpec((d, tn), lambda j: (0, j)),
                  pl.BlockSpec((1, tn), lambda j: (0, j))],
        out_specs=pl.BlockSpec((rows, tn), lambda j: (0, j)),
        out_shape=jax.ShapeDtypeStruct((rows, n), F32),
        compiler_params=_params(dimension_semantics=("parallel",)),
        name="ada_mod",
    )(c, w_ada, b_ada.reshape(1, n))


def _grouped_spec(tm, d, col=0):
    return pl.BlockSpec((None, tm, d), lambda b, r, *_: (b, r, col))


def _time_major_spec(tm, d):
    return pl.BlockSpec((tm, d), lambda b, r, *_: (r, b))


def _mod_spec(mod, tm, chunk):
    if mod.shape[1] == 1:
        return pl.BlockSpec((None, 1, D_MODEL), lambda b, r, *_: (b, 0, chunk))
    return pl.BlockSpec((None, tm, D_MODEL), lambda b, r, *_: (b, r, chunk))


def _const_spec(shape):
    nd = len(shape)
    return pl.BlockSpec(shape, lambda *_: (0,) * nd)


def _inproj_body(x_ref, sc_ref, sh_ref, w_ref, o_ref):
    h = (x_ref[...] * (1.0 + sc_ref[...]) + sh_ref[...]).astype(BF16)
    o_ref[...] = jnp.dot(h, w_ref[...], preferred_element_type=F32)


def _in_proj(x, mod, w_in_p):
    B, T, D = x.shape
    tm = min(512, T)
    nj = 2
    tn = NP_COLS // nj
    rows = lambda spec: pl.BlockSpec(spec.block_shape, lambda j, b, r: spec.index_map(b, r))
    return pl.pallas_call(
        _inproj_body,
        grid=(nj, B, T // tm),
        in_specs=[rows(_grouped_spec(tm, D)), rows(_mod_spec(mod, tm, 1)), rows(_mod_spec(mod, tm, 0)),
                  pl.BlockSpec((D, tn), lambda j, b, r: (0, j))],
        out_specs=pl.BlockSpec((None, tm, tn), lambda j, b, r: (b, r, j)),
        out_shape=jax.ShapeDtypeStruct((B, T, NP_COLS), F32),
        compiler_params=_params(dimension_semantics=("parallel", "parallel", "parallel")),
        name="in_proj",
    )(x, mod, mod, w_in_p)


SUBLANES = 8


def _rwkv_pre_body(p_ref, pb_ref, prev_ref, mu_ref, wa0_ref, wl_ref, g2_ref,
                   r_ref, k_ref, v_ref, w_ref, a_ref, g_ref, *, shift):
    p = p_ref[...]
    at_start = pl.program_id(1) == 0
    if shift == 1:
        first = jnp.where(at_start, prev_ref[...], pb_ref[SUBLANES - 1:SUBLANES, :])
        row = lax.broadcasted_iota(jnp.int32, p.shape, 0)
        shifted = jnp.where(row == 0, first, pltpu.roll(p, 1, axis=0))
    else:
        first = jnp.where(at_start, prev_ref[...], pb_ref[...])
        shifted = jnp.concatenate([first, p[:-shift]], axis=0)
    z = p + mu_ref[...] * (shifted - p)
    r_ref[...] = z[:, OFF_R:OFF_R + RWKV_WIDTH].T
    k_ref[...] = z[:, OFF_K:OFF_K + RWKV_WIDTH].T
    v_ref[...] = z[:, OFF_V:OFF_V + RWKV_WIDTH].T
    wa = z[:, OFF_WA:OFF_WA + 256]
    lane = lax.broadcasted_iota(jnp.int32, wa.shape, 1)
    lora_in = jnp.where(lane < DECAY_LORA, jnp.tanh(wa), wa).astype(BF16)
    pre = jnp.dot(lora_in, wl_ref[...], preferred_element_type=F32) + wa0_ref[...]
    u = pre[:, :RWKV_WIDTH]
    w_log = jnp.minimum(u, 0.0) - jnp.log1p(jnp.exp(-jnp.abs(u))) - 0.5
    w_ref[...] = jnp.exp(-jnp.exp(w_log)).T
    a_ref[...] = _sigmoid(pre[:, RWKV_WIDTH:]).T
    gd = _sigmoid(z[:, OFF_GD:OFF_GD + GATE_LORA]).astype(BF16)
    g_ref[...] = jnp.dot(gd, g2_ref[...], preferred_element_type=F32)


def _rwkv_pre(proj, prev_p, shift, mu_p, wa0, w_lora, g2):
    B, T, _ = proj.shape
    tm = min(256, T)
    back = max(shift, SUBLANES)
    assert tm % back == 0 and prev_p.shape[1] == shift
    blk = pl.BlockSpec((RWKV_WIDTH, tm), lambda b, r: (b, r))
    out = jax.ShapeDtypeStruct((B * RWKV_WIDTH, T), F32)
    return pl.pallas_call(
        functools.partial(_rwkv_pre_body, shift=shift),
        grid=(B, T // tm),
        in_specs=[_grouped_spec(tm, RW_COLS),
                  pl.BlockSpec((None, back, RW_COLS), lambda b, r: (b, jnp.maximum(r * (tm // back) - 1, 0), 0)),
                  pl.BlockSpec((None, shift, RW_COLS), lambda b, r: (b, 0, 0)),
                  _const_spec((1, RW_COLS)), _const_spec((1, 2 * RWKV_WIDTH)),
                  _const_spec((256, 2 * RWKV_WIDTH)), _const_spec((GATE_LORA, RWKV_WIDTH))],
        out_specs=[blk] * 5 + [_grouped_spec(tm, RWKV_WIDTH)],
        out_shape=[out] * 5 + [jax.ShapeDtypeStruct((B, T, RWKV_WIDTH), F32)],
        compiler_params=_params(dimension_semantics=("parallel", "parallel")),
        name="rwkv_pre",
    )(proj, proj, prev_p, mu_p, wa0, w_lora, g2)


WKV_UNROLL = 8


def _wkv_body(r_ref, k_ref, v_ref, w_ref, a_ref, kk_p, ka_p, rk_p, lg_p, lb_p, s0_ref,
              o_ref, sout_ref, S, b_kk, b_wr, b_kka, b_k2, b_rows):
    tt = r_ref.shape[0]
    N = RWKV_HEAD

    @pl.when(pl.program_id(1) == 0)
    def _():
        S[...] = s0_ref[...]

    r = r_ref[...]
    k = k_ref[...]
    a = a_ref[...]
    kk = k * kk_p[...]
    nrm = jnp.sqrt(jnp.sum(kk * kk, axis=1, keepdims=True))
    kk = kk / jnp.maximum(nrm, 1e-12)
    k2 = k * (1.0 + (a - 1.0) * ka_p[...])
    kka = kk * a
    b_kk[...] = kk
    b_wr[...] = w_ref[...] * r
    b_kka[...] = kka
    b_k2[...] = k2
    b_rows[:, 0:1, :] = jnp.sum(kka * r, axis=1, keepdims=True)
    b_rows[:, 1:2, :] = jnp.sum(k2 * r, axis=1, keepdims=True)
    b_rows[:, 2:3, :] = jnp.sum(r * k2 * rk_p[...], axis=1, keepdims=True)

    def tok(t, carry):
        def reduce_pass(j0, acc):
            sa, o1 = acc
            for jj in range(WKV_UNROLL):
                j = j0 * WKV_UNROLL + jj
                Sj = S[j]
                sa = sa + Sj * b_kk[t, pl.ds(j, 1), :]
                o1 = o1 + Sj * b_wr[t, pl.ds(j, 1), :]
            return sa, o1

        zero = jnp.zeros((N, LANES), F32)
        sa, o1 = lax.fori_loop(0, N // WKV_UNROLL, reduce_pass, (zero, zero))
        vT = v_ref[t]
        o_ref[t] = o1 - sa * b_rows[t, 0:1, :] + vT * b_rows[t, 1:2, :]

        def update_pass(j0, c):
            for jj in range(WKV_UNROLL):
                j = j0 * WKV_UNROLL + jj
                S[j] = (S[j] * w_ref[t, pl.ds(j, 1), :] - sa * b_kka[t, pl.ds(j, 1), :]
                        + vT * b_k2[t, pl.ds(j, 1), :])
            return c

        lax.fori_loop(0, N // WKV_UNROLL, update_pass, 0)
        return carry

    lax.fori_loop(0, tt, tok, 0)

    o = o_ref[...]
    mu = jnp.mean(o, axis=1, keepdims=True)
    oc = o - mu
    var = jnp.mean(oc * oc, axis=1, keepdims=True)
    on = oc * lax.rsqrt(var + GN_EPS) * lg_p[...] + lb_p[...]
    o_ref[...] = on + b_rows[:, 2:3, :] * v_ref[...]

    @pl.when(pl.program_id(1) == pl.num_programs(1) - 1)
    def _():
        sout_ref[...] = S[...]


def _wkv(r, k, v, w, a, tiles, s0):
    T, G, N, _ = r.shape
    tt = min(32, T)
    tok_spec = pl.BlockSpec((tt, None, N, LANES), lambda g_, i: (i, g_, 0, 0))
    st_spec = pl.BlockSpec((None, N, N, LANES), lambda g_, i: (g_, 0, 0, 0))
    per_group = tiles[0].shape[0] > 1
    tile_spec = pl.BlockSpec((None, N, LANES), lambda g_, i: (g_ if per_group else 0, 0, 0))
    return pl.pallas_call(
        _wkv_body,
        grid=(G, T // tt),
        in_specs=[tok_spec] * 5 + [tile_spec] * 5 + [st_spec],
        out_specs=[tok_spec, st_spec],
        out_shape=[jax.ShapeDtypeStruct((T, G, N, LANES), F32), jax.ShapeDtypeStruct((G, N, N, LANES), F32)],
        scratch_shapes=[pltpu.VMEM((N, N, LANES), F32)] + [pltpu.VMEM((tt, N, LANES), F32)] * 4
                       + [pltpu.VMEM((tt, SUBLANES, LANES), F32)],
        compiler_params=_params(dimension_semantics=("parallel", "arbitrary")),
        name="wkv_scan",
    )(r, k, v, w, a, *tiles, s0)


def _rot_half(x):
    n = x.shape[-1]
    lane = lax.broadcasted_iota(jnp.int32, x.shape, x.ndim - 1)
    fwd = pltpu.roll(x, n - MLA_ROPE // 2, axis=x.ndim - 1)
    bwd = pltpu.roll(x, MLA_ROPE // 2, axis=x.ndim - 1)
    return jnp.where(lane % MLA_ROPE < MLA_ROPE // 2, fwd, bwd)


def _mla_body(cq_ref, ckr_ref, cos_ref, sin_ref, qn_ref, wq_ref, wuk_ref, kvn_ref,
              q_ref, kc_ref, lat_ref, kr_ref, latt_ref):
    cq = cq_ref[...]
    q = jnp.dot(_rmsnorm(cq, qn_ref[...]).astype(BF16), wq_ref[...], preferred_element_type=F32)
    nope_w = MLA_HEADS * MLA_NOPE
    qr = q[:, nope_w:]
    q_rope = qr * cos_ref[...] + _rot_half(qr) * sin_ref[...]
    for h in range(MLA_HEADS):
        qn = q[:, h * MLA_NOPE:(h + 1) * MLA_NOPE].astype(BF16)
        q_ref[h, :, :KV_RANK] = jnp.dot(qn, wuk_ref[h], preferred_element_type=F32).astype(BF16)
        q_ref[h, :, KV_RANK:] = q_rope[:, h * MLA_ROPE:(h + 1) * MLA_ROPE].astype(BF16)
    ckr = ckr_ref[...]
    lat = _rmsnorm(ckr[:, :KV_RANK], kvn_ref[...])
    slab = ckr[:, KV_RANK:KV_RANK + LANES]
    kr = (slab * cos_ref[:, :LANES] + _rot_half(slab) * sin_ref[:, :LANES])[:, :MLA_ROPE]
    lat_ref[...] = lat
    kr_ref[...] = kr
    kc_ref[:, :KV_RANK] = lat.astype(BF16)
    kc_ref[:, KV_RANK:] = kr.astype(BF16)
    latt_ref[...] = lat.T.astype(BF16)


def _mla_proj(proj, cos_t, sin_t, q_norm, wq_p, wuk_t, kv_norm):
    B, T, _ = proj.shape
    tm = min(512, T)
    qk = KV_RANK + MLA_ROPE
    tab = pl.BlockSpec((None, tm, MLA_HEADS * MLA_ROPE), lambda b, r: (0, r, 0))
    return pl.pallas_call(
        _mla_body,
        grid=(B, T // tm),
        in_specs=[_grouped_spec(tm, MLA_BLK, OFF_CQ // MLA_BLK), _grouped_spec(tm, MLA_BLK, OFF_CKV // MLA_BLK),
                  tab, tab, _const_spec((1, Q_RANK)), _const_spec(wq_p.shape), _const_spec(wuk_t.shape),
                  _const_spec((1, KV_RANK))],
        out_specs=[pl.BlockSpec((None, MLA_HEADS, tm, qk), lambda b, r: (b, 0, r, 0)),
                   _grouped_spec(tm, qk), _grouped_spec(tm, KV_RANK), _grouped_spec(tm, MLA_ROPE),
                   pl.BlockSpec((None, KV_RANK, tm), lambda b, r: (b, 0, r))],
        out_shape=[jax.ShapeDtypeStruct((B, MLA_HEADS, T, qk), BF16),
                   jax.ShapeDtypeStruct((B, T, qk), BF16),
                   jax.ShapeDtypeStruct((B, T, KV_RANK), F32),
                   jax.ShapeDtypeStruct((B, T, MLA_ROPE), F32),
                   jax.ShapeDtypeStruct((B, KV_RANK, T), BF16)],
        compiler_params=_params(dimension_semantics=("parallel", "parallel")),
        name="mla_proj",
    )(proj, proj, cos_t, sin_t, q_norm, wq_p, wuk_t, kv_norm)


HEADS_PER_DOT = 4


def _pattn_body(q_ref, k_ref, vt_ref, o_ref, m_scr, l_scr, acc_scr, *, tq, tk):
    i = pl.program_id(1)
    j = pl.program_id(2)
    last_j = ((i + 1) * tq - 1) // tk

    @pl.when(j == 0)
    def _():
        m_scr[...] = jnp.full(m_scr.shape, NEG_BIG, F32)
        l_scr[...] = jnp.zeros(l_scr.shape, F32)
        acc_scr[...] = jnp.zeros(acc_scr.shape, F32)

    def step(masked):
        k = k_ref[...]
        vt = vt_ref[...]
        if masked:
            cols = HEADS_PER_DOT * tq
            kpos = j * tk + lax.broadcasted_iota(jnp.int32, (tk, cols), 0)
            qpos = i * tq + lax.broadcasted_iota(jnp.int32, (tk, cols), 1) % tq
            keep = kpos <= qpos
        for g in range(MLA_HEADS // HEADS_PER_DOT):
            q = q_ref[g * HEADS_PER_DOT:(g + 1) * HEADS_PER_DOT].reshape(HEADS_PER_DOT * tq, q_ref.shape[-1])
            st = lax.dot_general(k, q, (((1,), (1,)), ((), ())),
                                 preferred_element_type=F32) * ATTN_SCALE
            if masked:
                st = jnp.where(keep, st, NEG_BIG)
            m_prev = m_scr[g]
            m_new = jnp.maximum(m_prev, jnp.max(st, axis=0, keepdims=True))
            alpha = jnp.exp(m_prev - m_new)
            pt = jnp.exp(st - m_new)
            l_scr[g] = alpha * l_scr[g] + jnp.sum(pt, axis=0, keepdims=True)
            acc_scr[g] = alpha * acc_scr[g] + jnp.dot(vt, pt.astype(BF16), preferred_element_type=F32)
            m_scr[g] = m_new

    crosses_diagonal = (j + 1) * tk - 1 > i * tq
    pl.when(jnp.logical_and(j <= last_j, crosses_diagonal))(functools.partial(step, True))
    pl.when(jnp.logical_and(j <= last_j, jnp.logical_not(crosses_diagonal)))(functools.partial(step, False))

    @pl.when(j == pl.num_programs(2) - 1)
    def _():
        for g in range(MLA_HEADS // HEADS_PER_DOT):
            ctx_t = acc_scr[g] / l_scr[g]
            for hh in range(HEADS_PER_DOT):
                o_ref[g * HEADS_PER_DOT + hh] = ctx_t[:, hh * tq:(hh + 1) * tq].T.astype(BF16)


def _prompt_attn(q4, kc, lat_t):
    B, H, T, qk = q4.shape
    tq = min(256, T)
    tk = min(512, T)
    body = functools.partial(_pattn_body, tq=tq, tk=tk)
    kv_block = lambda i, j: jnp.minimum(j, ((i + 1) * tq - 1) // tk)
    return pl.pallas_call(
        body,
        grid=(B, T // tq, T // tk),
        in_specs=[pl.BlockSpec((None, H, tq, qk), lambda b, i, j: (b, 0, i, 0)),
                  pl.BlockSpec((None, tk, qk), lambda b, i, j: (b, kv_block(i, j), 0)),
                  pl.BlockSpec((None, KV_RANK, tk), lambda b, i, j: (b, 0, kv_block(i, j)))],
        out_specs=pl.BlockSpec((None, H, tq, KV_RANK), lambda b, i, j: (b, 0, i, 0)),
        out_shape=jax.ShapeDtypeStruct((B, H, T, KV_RANK), BF16),
        scratch_shapes=[pltpu.VMEM((H // HEADS_PER_DOT, 1, HEADS_PER_DOT * tq), F32),
                        pltpu.VMEM((H // HEADS_PER_DOT, 1, HEADS_PER_DOT * tq), F32),
                        pltpu.VMEM((H // HEADS_PER_DOT, KV_RANK, HEADS_PER_DOT * tq), F32)],
        compiler_params=_params(dimension_semantics=("parallel", "parallel", "arbitrary")),
        name="prompt_attn",
    )(q4, kc, lat_t)


SAMPLE_PAGES_PER_CHUNK = 64


def _sattn_body(pt_ref, q_ref, latn_ref, krn_ref, lat_hbm, kr_hbm, o_ref,
                lat_buf, kr_buf, sem, m_scr, l_scr, acc_scr, *, n_pg, n_chunks, n_new):
    seq = pl.program_id(0)
    nt = (((1,), (1,)), ((), ()))

    def page_copies(page_of, slot):
        for p in range(n_pg):
            page = page_of(p)
            yield pltpu.make_async_copy(lat_hbm.at[page], lat_buf.at[slot, p], sem.at[slot, 0])
            yield pltpu.make_async_copy(kr_hbm.at[page], kr_buf.at[slot, p], sem.at[slot, 1])

    def start_chunk(s, chunk, slot):
        for cp in page_copies(lambda p: pt_ref[s, chunk * n_pg + p], slot):
            cp.start()

    def wait_chunk(slot):
        for cp in page_copies(lambda p: 0, slot):
            cp.wait()

    @pl.when(seq == 0)
    def _():
        start_chunk(seq, 0, 0)

    m_scr[...] = jnp.full(m_scr.shape, NEG_BIG, F32)
    l_scr[...] = jnp.zeros(l_scr.shape, F32)
    acc_scr[...] = jnp.zeros(acc_scr.shape, F32)

    q = q_ref[...]
    q_lat = q[:, :KV_RANK]
    q_rope = q[:, KV_RANK:]

    def update(s, vals):
        m_prev = m_scr[...]
        m_new = jnp.maximum(m_prev, jnp.max(s, axis=1, keepdims=True))
        alpha = jnp.exp(m_prev - m_new)
        p = jnp.exp(s - m_new)
        l_scr[...] = alpha * l_scr[...] + jnp.sum(p, axis=1, keepdims=True)
        acc_scr[...] = alpha * acc_scr[...] + jnp.dot(p.astype(BF16), vals, preferred_element_type=F32)
        m_scr[...] = m_new

    for chunk in range(n_chunks):
        slot = chunk % 2
        if chunk + 1 < n_chunks:
            start_chunk(seq, chunk + 1, 1 - slot)
        else:
            @pl.when(seq + 1 < pl.num_programs(0))
            def _():
                start_chunk(seq + 1, 0, 1 - slot)
        wait_chunk(slot)
        lat_all = lat_buf[slot].reshape(n_pg * lat_buf.shape[2], KV_RANK).astype(BF16)
        krt_all = jnp.concatenate([kr_buf[slot, p].astype(BF16) for p in range(n_pg)], axis=1)
        s = lax.dot_general(q_lat, lat_all, nt, preferred_element_type=F32)
        s = (s + jnp.dot(q_rope, krt_all, preferred_element_type=F32)) * ATTN_SCALE
        update(s, lat_all)

    latn = latn_ref[...].astype(BF16)
    sn = lax.dot_general(q_lat, latn, nt, preferred_element_type=F32)
    sn = (sn + lax.dot_general(q_rope, krn_ref[...].astype(BF16), nt, preferred_element_type=F32)) * ATTN_SCALE
    row_t = lax.broadcasted_iota(jnp.int32, sn.shape, 0) % n_new
    col_t = lax.broadcasted_iota(jnp.int32, sn.shape, 1)
    sn = jnp.where(col_t <= row_t, sn, NEG_BIG)
    update(sn, latn)
    o_ref[...] = acc_scr[...] / l_scr[...]


def _sample_attn(q, latn, krn, cache_lat, cache_kr, page_table):
    S, rows, qk = q.shape
    n_pages = page_table.shape[1]
    page = cache_lat.shape[1]
    n_pg = min(SAMPLE_PAGES_PER_CHUNK, n_pages // 2)
    n_chunks = n_pages // n_pg
    assert n_pages % n_pg == 0 and n_chunks % 2 == 0
    n_new = rows // MLA_HEADS
    pad_new = latn.shape[1]
    grid_spec = pltpu.PrefetchScalarGridSpec(
        num_scalar_prefetch=1,
        grid=(S,),
        in_specs=[pl.BlockSpec((None, rows, qk), lambda s, pt: (s, 0, 0)),
                  pl.BlockSpec((None, pad_new, KV_RANK), lambda s, pt: (s, 0, 0)),
                  pl.BlockSpec((None, pad_new, MLA_ROPE), lambda s, pt: (s, 0, 0)),
                  pl.BlockSpec(memory_space=pl.ANY), pl.BlockSpec(memory_space=pl.ANY)],
        out_specs=pl.BlockSpec((None, rows, KV_RANK), lambda s, pt: (s, 0, 0)),
        scratch_shapes=[pltpu.VMEM((2, n_pg, page, KV_RANK), F32), pltpu.VMEM((2, n_pg, MLA_ROPE, page), F32),
                        pltpu.SemaphoreType.DMA((2, 2)),
                        pltpu.VMEM((rows, 1), F32), pltpu.VMEM((rows, 1), F32),
                        pltpu.VMEM((rows, KV_RANK), F32)],
    )
    return pl.pallas_call(
        functools.partial(_sattn_body, n_pg=n_pg, n_chunks=n_chunks, n_new=n_new),
        grid_spec=grid_spec,
        out_shape=jax.ShapeDtypeStruct((S, rows, KV_RANK), F32),
        compiler_params=_params(dimension_semantics=("arbitrary",)),
        name="sample_attn",
    )(page_table, q, latn, krn, cache_lat, cache_kr)


def _uv_body(ctx_ref, wuv_ref, on_ref, o_ref):
    parts = [jnp.dot(ctx_ref[h].astype(BF16), wuv_ref[h], preferred_element_type=F32)
             for h in range(MLA_HEADS)]
    om = jnp.concatenate(parts, axis=1)
    o_ref[...] = _rmsnorm(om, on_ref[...]).astype(BF16)


def _uv_norm(ctx, wuv_t, out_norm):
    B, H, T, R = ctx.shape
    tm = min(512, T)
    return pl.pallas_call(
        _uv_body,
        grid=(B, T // tm),
        in_specs=[pl.BlockSpec((None, H, tm, R), lambda b, r: (b, 0, r, 0)),
                  _const_spec(wuv_t.shape), _const_spec((1, MLA_WIDTH))],
        out_specs=_grouped_spec(tm, MLA_WIDTH),
        out_shape=jax.ShapeDtypeStruct((B, T, MLA_WIDTH), BF16),
        compiler_params=_params(dimension_semantics=("parallel", "parallel")),
        name="uv_norm",
    )(ctx, wuv_t, out_norm)


def _outproj_body(or_ref, gate_ref, om_ref, x_ref, g1_ref, w_ref, lg_ref, lb_ref, o_ref):
    o_r = (or_ref[...] * gate_ref[...]).astype(BF16)
    mixed = jnp.dot(o_r, w_ref[:RWKV_WIDTH, :], preferred_element_type=F32)
    mixed = mixed + jnp.dot(om_ref[...], w_ref[RWKV_WIDTH:, :], preferred_element_type=F32)
    y = DEEPNORM_ALPHA * x_ref[...] + g1_ref[...] * mixed
    o_ref[...] = _layernorm(y, lg_ref[...], lb_ref[...])


def _out_proj(o_r, gate, o_m, x, mod, w_out, ln_g, ln_b):
    B, T, D = x.shape
    tm = min(512, T)
    return pl.pallas_call(
        _outproj_body,
        grid=(B, T // tm),
        in_specs=[_time_major_spec(tm, RWKV_WIDTH), _grouped_spec(tm, RWKV_WIDTH),
                  _grouped_spec(tm, MLA_WIDTH), _grouped_spec(tm, D),
                  _mod_spec(mod, tm, 2), _const_spec((D, D)), _const_spec((1, D)), _const_spec((1, D))],
        out_specs=_grouped_spec(tm, D),
        out_shape=jax.ShapeDtypeStruct((B, T, D), F32),
        compiler_params=_params(dimension_semantics=("parallel", "parallel")),
        name="out_proj_ln1",
    )(o_r, gate, o_m, x, mod, w_out, ln_g, ln_b)


def _ffn_body(x_ref, sc_ref, sh_ref, g2_ref, wu_ref, wd_ref, lg_ref, lb_ref, o_ref, h_scr, acc_scr):
    f = pl.program_id(2)

    @pl.when(f == 0)
    def _():
        h_scr[...] = (x_ref[...] * (1.0 + sc_ref[...]) + sh_ref[...]).astype(BF16)
        acc_scr[...] = jnp.zeros(acc_scr.shape, F32)

    u = jnp.maximum(jnp.dot(h_scr[...], wu_ref[...], preferred_element_type=F32), 0.0)
    acc_scr[...] += jnp.dot((u * u).astype(BF16), wd_ref[...], preferred_element_type=F32)

    @pl.when(f == pl.num_programs(2) - 1)
    def _():
        y = DEEPNORM_ALPHA * x_ref[...] + g2_ref[...] * acc_scr[...]
        o_ref[...] = _layernorm(y, lg_ref[...], lb_ref[...])


def _ffn(x, mod, w_up, w_down, ln_g, ln_b):
    B, T, D = x.shape
    tm = min(512, T)
    tf = 1024
    return pl.pallas_call(
        _ffn_body,
        grid=(B, T // tm, D_FF // tf),
        in_specs=[_grouped_spec(tm, D), _mod_spec(mod, tm, 4), _mod_spec(mod, tm, 3), _mod_spec(mod, tm, 5),
                  pl.BlockSpec((D, tf), lambda b, r, f: (0, f)),
                  pl.BlockSpec((tf, D), lambda b, r, f: (f, 0)),
                  _const_spec((1, D)), _const_spec((1, D))],
        out_specs=_grouped_spec(tm, D),
        out_shape=jax.ShapeDtypeStruct((B, T, D), F32),
        scratch_shapes=[pltpu.VMEM((tm, D), BF16), pltpu.VMEM((tm, D), F32)],
        compiler_params=_params(dimension_semantics=("parallel", "parallel", "arbitrary")),
        name="ffn_ln2",
    )(x, mod, mod, mod, w_up, w_down, ln_g, ln_b)


def _rope_tables(pos):
    half = MLA_ROPE // 2
    inv = ROPE_THETA ** (-jnp.arange(half, dtype=F32) / half)
    ang = pos.astype(F32)[:, None] * inv
    cos, sin = jnp.cos(ang), jnp.sin(ang)
    cos_t = jnp.tile(jnp.concatenate([cos, cos], -1), (1, MLA_HEADS))
    sin_t = jnp.tile(jnp.concatenate([-sin, sin], -1), (1, MLA_HEADS))
    return cos_t[None], sin_t[None]


SEQ_PER_GROUP = LANES // RWKV_HEADS


class _PromptLanes:
    @staticmethod
    def to_lanes(x, n_seq):
        T = x.shape[1]
        return x.reshape(n_seq // SEQ_PER_GROUP, LANES, RWKV_HEAD, T).transpose(3, 0, 2, 1)

    @staticmethod
    def from_lanes(o):
        T, G = o.shape[:2]
        return o.transpose(0, 1, 3, 2).reshape(T, G * LANES * RWKV_HEAD)

    @staticmethod
    def tile(p, n_seq):
        return jnp.tile(p.reshape(RWKV_HEADS, RWKV_HEAD).T, (1, SEQ_PER_GROUP))[None]

    @staticmethod
    def state_to_lanes(s):
        B = s.shape[0]
        s = s.reshape(B // SEQ_PER_GROUP, LANES, RWKV_HEAD, RWKV_HEAD)
        return s.transpose(0, 3, 2, 1)

    @staticmethod
    def state_from_lanes(s):
        G = s.shape[0]
        return s.transpose(0, 3, 2, 1).reshape(G * SEQ_PER_GROUP, RWKV_HEADS, RWKV_HEAD, RWKV_HEAD)


class _SampleLanes:
    @staticmethod
    def to_lanes(x, n_seq):
        x = x.reshape(RWKV_HEADS, RWKV_HEAD, -1, n_seq // LANES, LANES)
        return x.transpose(2, 3, 0, 1, 4).reshape(x.shape[2], (n_seq // LANES) * RWKV_HEADS, RWKV_HEAD, LANES)

    @staticmethod
    def from_lanes(o):
        Td, G = o.shape[:2]
        o = o.reshape(Td, G // RWKV_HEADS, RWKV_HEADS, RWKV_HEAD, LANES)
        return o.transpose(0, 1, 4, 2, 3).reshape(Td * (G // RWKV_HEADS) * LANES, RWKV_WIDTH)

    @staticmethod
    def tile(p, n_seq):
        t = jnp.broadcast_to(p.reshape(RWKV_HEADS, RWKV_HEAD, 1), (RWKV_HEADS, RWKV_HEAD, LANES))
        return jnp.tile(t, (n_seq // LANES, 1, 1))

    @staticmethod
    def state_to_lanes(s):
        S = s.shape[0]
        s = s.reshape(S // LANES, LANES, RWKV_HEADS, RWKV_HEAD, RWKV_HEAD)
        return s.transpose(0, 2, 4, 3, 1).reshape((S // LANES) * RWKV_HEADS, RWKV_HEAD, RWKV_HEAD, LANES)

    @staticmethod
    def state_from_lanes(s):
        G = s.shape[0]
        s = s.reshape(G // RWKV_HEADS, RWKV_HEADS, RWKV_HEAD, RWKV_HEAD, LANES)
        return s.transpose(0, 4, 1, 3, 2).reshape((G // RWKV_HEADS) * LANES, RWKV_HEADS, RWKV_HEAD, RWKV_HEAD)


def _rwkv_inputs(proj, prev_p, n_seq, lanes, wp):
    r, k, v, w, a, g = _rwkv_pre(proj, prev_p, prev_p.shape[1], wp['mu'], wp['wa0'], wp['w_lora'], wp['g2'])
    return tuple(lanes.to_lanes(t, n_seq) for t in (r, k, v, w, a)), g


def _rwkv_scan(lane_inputs, state, lanes, wp):
    tiles = [lanes.tile(p, state.shape[0]) for p in wp['head_params']]
    o, s_out = _wkv(*lane_inputs, tiles, lanes.state_to_lanes(state))
    return lanes.from_lanes(o), lanes.state_from_lanes(s_out)


def _layer_back(x, mod, o_r, gate, ctx, wp):
    o_m = _uv_norm(ctx, wp['wuv'], wp['out_norm'])
    x1 = _out_proj(o_r, gate, o_m, x, mod, wp['w_out'], wp['ln1_g'], wp['ln1_b'])
    return _ffn(x1, mod, wp['w_up'], wp['w_down'], wp['ln2_g'], wp['ln2_b'])


def kernel(x_prompt, x_sample, c_prompt, c_sample, cache_latent, cache_krope, state_wkv, state_shift, page_table, w_ada, b_ada, w_in, rwkv_mu, rwkv_w0, rwkv_w2, rwkv_a0, rwkv_a2, rwkv_g2, rwkv_k_k, rwkv_k_a, rwkv_r_k, rwkv_lnx_g, rwkv_lnx_b, mla_q_norm, mla_w_q_up, mla_kv_norm, mla_w_uk, mla_w_uv, mla_out_norm, w_out, ln1_g, ln1_b, w_up, w_down, ln2_g, ln2_b):
    B, T, D = x_prompt.shape
    S, Td, _ = x_sample.shape
    past = page_table.shape[1] * cache_latent.shape[2]
    l = 0

    w_lora = jnp.zeros((256, 2 * RWKV_WIDTH), F32)
    w_lora = w_lora.at[:DECAY_LORA, :RWKV_WIDTH].set(rwkv_w2[l])
    w_lora = w_lora.at[DECAY_LORA:DECAY_LORA + AAA_LORA, RWKV_WIDTH:].set(rwkv_a2[l])
    wq = mla_w_q_up[l].reshape(Q_RANK, MLA_HEADS, MLA_NOPE + MLA_ROPE)
    wq_p = jnp.concatenate([wq[:, :, :MLA_NOPE].reshape(Q_RANK, -1), wq[:, :, MLA_NOPE:].reshape(Q_RANK, -1)], -1)
    row = lambda p: p.reshape(1, -1)
    wp = dict(
        w_in=_perm_cols(w_in[l]).astype(BF16),
        mu=row(_perm_cols(rwkv_mu[l])),
        wa0=row(jnp.concatenate([rwkv_w0[l], rwkv_a0[l]])),
        w_lora=w_lora.astype(BF16),
        g2=rwkv_g2[l].astype(BF16),
        head_params=[rwkv_k_k[l], rwkv_k_a[l], rwkv_r_k[l].reshape(-1), rwkv_lnx_g[l], rwkv_lnx_b[l]],
        q_norm=row(mla_q_norm[l]), wq=wq_p.astype(BF16),
        wuk=mla_w_uk[l].transpose(1, 2, 0).astype(BF16),
        kv_norm=row(mla_kv_norm[l]),
        wuv=mla_w_uv[l].transpose(1, 0, 2).astype(BF16),
        out_norm=row(mla_out_norm[l]),
        w_out=w_out[l].astype(BF16), ln1_g=row(ln1_g[l]), ln1_b=row(ln1_b[l]),
        w_up=w_up[l].astype(BF16), w_down=w_down[l].astype(BF16), ln2_g=row(ln2_g[l]), ln2_b=row(ln2_b[l]),
    )

    mod = _ada(jnp.concatenate([c_prompt, c_sample], 0), w_ada[l], b_ada[l])
    mod_p = mod[:B, None, :]
    mod_s = jnp.tile(mod[B:], (Td, 1))[None]

    s0_p = jnp.zeros((B, RWKV_HEADS, RWKV_HEAD, RWKV_HEAD), F32)
    prev0 = jnp.zeros((B, 1, RW_COLS), F32)
    proj_p = _in_proj(x_prompt, mod_p, wp['w_in'])
    lane_in_p, gate_p = _rwkv_inputs(proj_p, prev0, B, _PromptLanes, wp)
    cos_t, sin_t = _rope_tables(jnp.arange(T))
    q4, kc, lat_p, kr_p, lat_t = _mla_proj(proj_p, cos_t, sin_t, wp['q_norm'], wp['wq'], wp['wuk'],
                                           wp['kv_norm'])
    last_p = _unperm_shift_cols(proj_p[:, T - 1, :RW_COLS])
    ctx_p = _prompt_attn(q4, kc, lat_t)
    lane_in_p, ctx_p = lax.optimization_barrier((lane_in_p, ctx_p))
    o_r, wkv_p = _rwkv_scan(lane_in_p, s0_p, _PromptLanes, wp)
    y_p = _layer_back(x_prompt, mod_p, o_r, gate_p, ctx_p, wp)

    xs = x_sample.transpose(1, 0, 2).reshape(1, Td * S, D)
    pos_s = jnp.repeat(past + jnp.arange(Td), S)
    prev_s = _perm_cols(state_shift[l])[None]
    proj = _in_proj(xs, mod_s, wp['w_in'])
    lane_in_s, gate_s = _rwkv_inputs(proj, prev_s, S, _SampleLanes, wp)
    o_rs, wkv_s = _rwkv_scan(lane_in_s, state_wkv[l], _SampleLanes, wp)
    cos_t, sin_t = _rope_tables(pos_s)
    q4s, _, lat_s, kr_s, _ = _mla_proj(proj, cos_t, sin_t, wp['q_norm'], wp['wq'], wp['wuk'], wp['kv_norm'])
    last_s = _unperm_shift_cols(proj[0, (Td - 1) * S:, :RW_COLS])
    qk = KV_RANK + MLA_ROPE
    q_s = q4s.reshape(MLA_HEADS, Td, S, qk).transpose(2, 0, 1, 3).reshape(S, MLA_HEADS * Td, qk)
    lat_s = lat_s.reshape(Td, S, KV_RANK).transpose(1, 0, 2)
    kr_s = kr_s.reshape(Td, S, MLA_ROPE).transpose(1, 0, 2)
    pad16 = lambda t: jnp.pad(t, ((0, 0), (0, 16 - Td), (0, 0)))
    ctx_s = _sample_attn(q_s, pad16(lat_s), pad16(kr_s), cache_latent[l], cache_krope[l].transpose(0, 2, 1),
                         page_table)
    ctx_s = ctx_s.reshape(S, MLA_HEADS, Td, KV_RANK).transpose(1, 2, 0, 3).reshape(1, MLA_HEADS, Td * S, KV_RANK)
    y_s = _layer_back(xs, mod_s, o_rs, gate_s, ctx_s, wp)
    y_s = y_s.reshape(Td, S, D).transpose(1, 0, 2)

    return (y_p, y_s, lat_p[None], kr_p[None], wkv_p[None], last_p[None],
            lat_s[None], kr_s[None], wkv_s[None], last_s[None])
```
